```python
import math
import jax, jax.numpy as jnp
from jax import lax
import numpy as np

D_MODEL = 1024
BATCH = 4
SEQ = 8192
DEPTH = 4

HEAD_DIM = 64
A_HEADS = 4
B_Q_HEADS = 8
B_KV_HEADS = 2
C_HEADS = 16
ROT_DIM = HEAD_DIM // 4
ROPE_THETA = 500000.0
AXIAL_THETA = 10000.0
GRID_W = 64
NA_KH = 8
NA_KW = 16
Q_BLOCK = 128
D_FF = 2816
CONV_W = 3
PLE_DIM = 256
EPS = 1e-6

N_EVEN = (DEPTH + 1) // 2
N_ODD = DEPTH // 2
A_QK = A_HEADS * 2 * HEAD_DIM
A_V = A_HEADS * 2 * HEAD_DIM
B_Q = B_Q_HEADS * HEAD_DIM
B_KV = B_KV_HEADS * HEAD_DIM
EVEN_IN = 2 * A_QK + A_V + B_Q + 2 * B_KV
EVEN_SPLITS = [A_QK, 2 * A_QK, 2 * A_QK + A_V, 2 * A_QK + A_V + B_Q, 2 * A_QK + A_V + B_Q + B_KV]
EVEN_MIX = A_V + B_Q
C_IN = 3 * C_HEADS * HEAD_DIM
C_MIX = C_HEADS * HEAD_DIM

kernel_name = "hybrid_diff_gqa_axial_natten_convffn_ple"


def rms_norm(x, g):
    xf = x.astype(jnp.float32)
    y = xf * lax.rsqrt(jnp.mean(xf * xf, axis=-1, keepdims=True) + EPS)
    return (y * g.astype(jnp.float32)).astype(x.dtype)


def rotary(x, pos, theta):
    d = x.shape[-1]
    half = d // 2
    inv = theta ** (-jnp.arange(half, dtype=jnp.float32) / half)
    ang = pos.astype(jnp.float32)[:, None] * inv[None, :]
    shp = (1, x.shape[1]) + (1,) * (x.ndim - 3) + (half,)
    cos = jnp.cos(ang).reshape(shp)
    sin = jnp.sin(ang).reshape(shp)
    xf = x.astype(jnp.float32)
    x1, x2 = xf[..., :half], xf[..., half:]
    return jnp.concatenate([x1 * cos - x2 * sin, x2 * cos + x1 * sin], axis=-1).astype(x.dtype)


def partial_rope(x, pos):
    return jnp.concatenate([rotary(x[..., :ROT_DIM], pos, ROPE_THETA), x[..., ROT_DIM:]], axis=-1)


def axial_rope(x, row, col):
    half = x.shape[-1] // 2
    return jnp.concatenate([rotary(x[..., :half], row, AXIAL_THETA),
                            rotary(x[..., half:], col, AXIAL_THETA)], axis=-1)


def to_blocks(t):
    b, s = t.shape[:2]
    return jnp.moveaxis(t.reshape((b, s // Q_BLOCK, Q_BLOCK) + t.shape[2:]), 1, 0)


def from_blocks(t):
    t = jnp.moveaxis(t, 0, 1)
    return t.reshape((t.shape[0], t.shape[1] * t.shape[2]) + t.shape[3:])


def diff_attention(q, k, v, lam):
    scale = HEAD_DIM ** -0.5

    def block(qb):
        s = jnp.einsum('bqhjd,bkhjd->bhjqk', qb, k).astype(jnp.float32) * scale
        pr = jax.nn.softmax(s, axis=-1)
        w = (pr[:, :, 0] - lam * pr[:, :, 1]).astype(v.dtype)
        return jnp.einsum('bhqk,bkhe->bqhe', w, v)

    return from_blocks(lax.map(block, to_blocks(q)))


def gqa_attention(q, k, v):
    scale = HEAD_DIM ** -0.5

    def block(qb):
        s = jnp.einsum('bqgrd,bkgd->bgrqk', qb, k).astype(jnp.float32) * scale
        pr = jax.nn.softmax(s, axis=-1).astype(v.dtype)
        return jnp.einsum('bgrqk,bkgd->bqgrd', pr, v)

    return from_blocks(lax.map(block, to_blocks(q)))


def neighbourhood_attention(q, k, v, bias_table):
    b, s, h, d = q.shape
    rows = s // GRID_W
    kh = min(NA_KH, rows)
    kw = NA_KW
    scale = d ** -0.5
    qg = q.reshape(b, rows, GRID_W, h, d)
    kg = k.reshape(b, rows, GRID_W, h, d)
    vg = v.reshape(b, rows, GRID_W, h, d)
    r_idx = jnp.arange(rows, dtype=jnp.int32)
    row_start = jnp.clip(r_idx - kh // 2, 0, rows - kh)
    cols = jnp.arange(GRID_W, dtype=jnp.int32)
    col_start = jnp.clip(cols - kw // 2, 0, GRID_W - kw)
    col_valid = (cols[None, :] >= col_start[:, None]) & (cols[None, :] < col_start[:, None] + kw)
    dc_idx = jnp.clip(cols[None, :] - cols[:, None] + NA_KW - 1, 0, 2 * NA_KW - 2)

    def row_block(args):
        r, rs, qr = args
        kr = lax.dynamic_slice_in_dim(kg, rs, kh, axis=1)
        vr = lax.dynamic_slice_in_dim(vg, rs, kh, axis=1)
        sc = jnp.einsum('bqhd,bjkhd->bhqjk', qr, kr).astype(jnp.float32) * scale
        dr_idx = rs + jnp.arange(kh, dtype=jnp.int32) - r + NA_KH - 1
        bias = bias_table[:, dr_idx[None, :, None], dc_idx[:, None, :]]
        sc = sc + bias.astype(jnp.float32)[None]
        sc = jnp.where(col_valid[:, None, :], sc, -jnp.inf)
        pr = jax.nn.softmax(sc, axis=(-2, -1)).astype(vr.dtype)
        return jnp.einsum('bhqjk,bjkhd->bqhd', pr, vr)

    out = lax.map(row_block, (r_idx, row_start, jnp.moveaxis(qg, 1, 0)))
    return jnp.moveaxis(out, 0, 1).reshape(b, s, h * d)


def even_mixer(h, w_in, lq1, lk1, lq2, lk2, subln, q_norm, k_norm, w_out, lam_init, pos, row, col):
    b, s, _ = h.shape
    a_q, a_k, a_v, b_q, b_k, b_v = jnp.split(h @ w_in, EVEN_SPLITS, axis=-1)
    a_q = partial_rope(a_q.reshape(b, s, 2 * A_HEADS, HEAD_DIM), pos).reshape(b, s, A_HEADS, 2, HEAD_DIM)
    a_k = partial_rope(a_k.reshape(b, s, 2 * A_HEADS, HEAD_DIM), pos).reshape(b, s, A_HEADS, 2, HEAD_DIM)
    a_v = a_v.reshape(b, s, A_HEADS, 2 * HEAD_DIM)
    f32 = jnp.float32
    lam = (jnp.exp(jnp.sum(lq1.astype(f32) * lk1.astype(f32)))
           - jnp.exp(jnp.sum(lq2.astype(f32) * lk2.astype(f32))) + lam_init)
    a_o = rms_norm(diff_attention(a_q, a_k, a_v, lam), subln) * (1.0 - lam_init)
    b_q = axial_rope(rms_norm(b_q.reshape(b, s, B_Q_HEADS, HEAD_DIM), q_norm), row, col)
    b_k = axial_rope(rms_norm(b_k.reshape(b, s, B_KV_HEADS, HEAD_DIM), k_norm), row, col)
    b_q = b_q.reshape(b, s, B_KV_HEADS, B_Q_HEADS // B_KV_HEADS, HEAD_DIM)
    b_v = b_v.reshape(b, s, B_KV_HEADS, HEAD_DIM)
    b_o = gqa_attention(b_q, b_k, b_v)
    mix = jnp.concatenate([a_o.reshape(b, s, A_V), b_o.reshape(b, s, B_Q)], axis=-1)
    return mix @ w_out


def odd_mixer(h, w_in, rel_bias, w_out):
    b, s, _ = h.shape
    q, k, v = jnp.split((h @ w_in).reshape(b, s, 3, C_HEADS, HEAD_DIM), 3, axis=2)
    o = neighbourhood_attention(q[:, :, 0], k[:, :, 0], v[:, :, 0], rel_bias)
    return o @ w_out


def conv_ffn(h, w_up, conv_w, conv_b, w_down):
    u = h @ w_up
    up = jnp.pad(u, ((0, 0), (1, 1), (0, 0)))
    u = up[:, :-2] * conv_w[0] + up[:, 1:-1] * conv_w[1] + up[:, 2:] * conv_w[2] + conv_b
    gate, val = jnp.split(u, 2, axis=-1)
    return (jax.nn.silu(gate) * val) @ w_down


def setup_inputs(seed: int = 0) -> dict:
    key = jax.random.key(seed)
    ks = jax.random.split(key, 26)
    nrm = lambda k, shape, scale: jax.random.normal(k, shape, jnp.float32) * scale
    gain = lambda k, shape: 1.0 + 0.02 * jax.random.normal(k, shape, jnp.float32)
    return {
        "x": nrm(ks[0], (BATCH, SEQ, D_MODEL), 1.0),
        "p": nrm(ks[1], (DEPTH, BATCH, SEQ, PLE_DIM), 1.0),
        "attn_norm": gain(ks[2], (DEPTH, D_MODEL)),
        "w_in_ab": nrm(ks[3], (N_EVEN, D_MODEL, EVEN_IN), D_MODEL ** -0.5),
        "lambda_q1": nrm(ks[4], (N_EVEN, HEAD_DIM), 0.1),
        "lambda_k1": nrm(ks[5], (N_EVEN, HEAD_DIM), 0.1),
        "lambda_q2": nrm(ks[6], (N_EVEN, HEAD_DIM), 0.1),
        "lambda_k2": nrm(ks[7], (N_EVEN, HEAD_DIM), 0.1),
        "a_subln": gain(ks[8], (N_EVEN, 2 * HEAD_DIM)),
        "b_q_norm": gain(ks[9], (N_EVEN, HEAD_DIM)),
        "b_k_norm": gain(ks[10], (N_EVEN, HEAD_DIM)),
        "w_out_ab": nrm(ks[11], (N_EVEN, EVEN_MIX, D_MODEL), EVEN_MIX ** -0.5),
        "w_in_c": nrm(ks[12], (N_ODD, D_MODEL, C_IN), D_MODEL ** -0.5),
        "c_rel_bias": nrm(ks[13], (N_ODD, C_HEADS, 2 * NA_KH - 1, 2 * NA_KW - 1), 0.02),
        "w_out_c": nrm(ks[14], (N_ODD, C_MIX, D_MODEL), C_MIX ** -0.5),
        "ffn_norm": gain(ks[15], (DEPTH, D_MODEL)),
        "w_ffn_up": nrm(ks[16], (DEPTH, D_MODEL, 2 * D_FF), D_MODEL ** -0.5),
        "ffn_conv_w": nrm(ks[17], (DEPTH, CONV_W, 2 * D_FF), CONV_W ** -0.5),
        "ffn_conv_b": nrm(ks[18], (DEPTH, 2 * D_FF), 0.02),
        "w_ffn_down": nrm(ks[19], (DEPTH, D_FF, D_MODEL), D_FF ** -0.5),
        "ple_norm": gain(ks[20], (DEPTH, D_MODEL)),
        "w_ple_gate": nrm(ks[21], (DEPTH, D_MODEL, D_MODEL), D_MODEL ** -0.5),
        "w_ple_proj": nrm(ks[22], (DEPTH, PLE_DIM, D_MODEL), PLE_DIM ** -0.5),
        "final_norm": gain(ks[23], (D_MODEL,)),
    }


def reference(x, p, attn_norm, w_in_ab, lambda_q1, lambda_k1, lambda_q2, lambda_k2, a_subln,
              b_q_norm, b_k_norm, w_out_ab, w_in_c, c_rel_bias, w_out_c, ffn_norm, w_ffn_up,
              ffn_conv_w, ffn_conv_b, w_ffn_down, ple_norm, w_ple_gate, w_ple_proj, final_norm):
    s = x.shape[1]
    pos = jnp.arange(s, dtype=jnp.int32)
    row = pos // GRID_W
    col = pos % GRID_W
    for i in range(DEPTH):
        j = i // 2
        h = rms_norm(x, attn_norm[i])
        if i % 2 == 0:
            lam_init = 0.8 - 0.6 * math.exp(-0.3 * i)
            y = even_mixer(h, w_in_ab[j], lambda_q1[j], lambda_k1[j], lambda_q2[j], lambda_k2[j],
                           a_subln[j], b_q_norm[j], b_k_norm[j], w_out_ab[j], lam_init, pos, row, col)
        else:
            y = odd_mixer(h, w_in_c[j], c_rel_bias[j], w_out_c[j])
        x = x + y
        x = x + conv_ffn(rms_norm(x, ffn_norm[i]), w_ffn_up[i], ffn_conv_w[i], ffn_conv_b[i], w_ffn_down[i])
        gate = jax.nn.sigmoid(rms_norm(x, ple_norm[i]) @ w_ple_gate[i])
        x = x + gate * (p[i] @ w_ple_proj[i])
    return rms_norm(x, final_norm)
```

```python
import functools
import math

import jax
import jax.numpy as jnp
from jax import lax
from jax.experimental import pallas as pl
from jax.experimental.pallas import tpu as pltpu

F32 = jnp.float32
BF16 = jnp.bfloat16

D_MODEL = 1024
HEAD_DIM = 64
A_HEADS = 4
B_Q_HEADS = 8
B_KV_HEADS = 2
C_HEADS = 16
ROPE_THETA = 500000.0
AXIAL_THETA = 10000.0
GRID_W = 64
NA_KH = 8
NA_KW = 16
D_FF = 2816
PLE_DIM = 256
EPS = 1e-6
SCALE = HEAD_DIM ** -0.5

A_QK = A_HEADS * 2 * HEAD_DIM
A_V = A_HEADS * 2 * HEAD_DIM
B_Q = B_Q_HEADS * HEAD_DIM
B_KV = B_KV_HEADS * HEAD_DIM
EVEN_IN = 2 * A_QK + A_V + B_Q + 2 * B_KV
C_MIX = C_HEADS * HEAD_DIM

TM = 512
TQ_A = 256
TQ_B = 128
TK = 512
TF = 256
HALO = 8
NA_ROWS = 4
NA_WIN = NA_ROWS + NA_KH - 1
NA_PT_OFF = NA_ROWS - 2
NA_PT = NA_WIN + NA_KH - 1 + NA_PT_OFF
VMEM_LIMIT = 52 * 1024 * 1024

_NT = (((1,), (1,)), ((), ()))


def _dot(a, b):
    return jnp.dot(a, b, preferred_element_type=F32)


def _rms(xf, g):
    return xf * lax.rsqrt(jnp.mean(xf * xf, axis=-1, keepdims=True) + EPS) * g


def _params(n_axes, limit=VMEM_LIMIT):
    return pltpu.CompilerParams(dimension_semantics=("arbitrary",) * n_axes, vmem_limit_bytes=limit)


def _resident(shape):
    nd = len(shape)
    return pl.BlockSpec(shape, lambda *_: (0,) * nd, pipeline_mode=pl.Buffered(1))


def _proj_even_kernel(x_ref, g_ref, wt_ref, ca_ref, sa_ref, cr_ref, sr_ref, cc_ref, sc_ref, qn_ref, kn_ref,
                      qta_ref, ka_ref, vta_ref, qtb_ref, kb_ref, vtb_ref):
    h = _rms(x_ref[...], g_ref[...]).astype(BF16)

    def proj_t(lo, hi):
        return lax.dot_general(wt_ref[lo:hi, :], h, _NT, preferred_element_type=F32)

    ca, sa = ca_ref[...], sa_ref[...]
    cr, sr = cr_ref[...], sr_ref[...]
    cc, sc = cc_ref[...], sc_ref[...]

    def rope_a(blk):
        x1, x2 = blk[0:8], blk[8:16]
        return jnp.concatenate([x1 * ca - x2 * sa, x2 * ca + x1 * sa, blk[16:]], axis=0)

    def norm_rope_b(blk, gain):
        n = blk * lax.rsqrt(jnp.mean(blk * blk, axis=0, keepdims=True) + EPS) * gain
        r1, r2, c1, c2 = n[0:16], n[16:32], n[32:48], n[48:64]
        return jnp.concatenate([r1 * cr - r2 * sr, r2 * cr + r1 * sr,
                                c1 * cc - c2 * sc, c2 * cc + c1 * sc], axis=0)

    o = 0
    aq = proj_t(o, o + A_QK)
    qta_ref[0] = (jnp.concatenate([rope_a(aq[m * 64:(m + 1) * 64]) for m in range(2 * A_HEADS)], axis=0)
                  * SCALE).astype(BF16)
    o += A_QK
    ak = proj_t(o, o + A_QK)
    for hh in range(A_HEADS):
        kt = jnp.concatenate([rope_a(ak[(2 * hh + m) * 64:(2 * hh + m + 1) * 64]) for m in range(2)], axis=0)
        ka_ref[0, hh] = kt.T.astype(BF16)
    o += A_QK
    vta_ref[0] = proj_t(o, o + A_V).astype(BF16)
    o += A_V
    bq = proj_t(o, o + B_Q)
    qn = qn_ref[...]
    qtb_ref[0] = (jnp.concatenate([norm_rope_b(bq[m * 64:(m + 1) * 64], qn) for m in range(B_Q_HEADS)], axis=0)
                  * SCALE).astype(BF16)
    o += B_Q
    bk = proj_t(o, o + B_KV)
    kn = kn_ref[...]
    kbt = jnp.concatenate([norm_rope_b(bk[m * 64:(m + 1) * 64], kn) for m in range(B_KV_HEADS)], axis=0)
    kb_ref[0] = kbt.T.astype(BF16)
    o += B_KV
    vtb_ref[0] = proj_t(o, o + B_KV).astype(BF16)


def _proj_even(xf, g, wt, tabs, qn, kn, batch, seq):
    t = xf.shape[0]
    ns = seq // TM
    tok = lambda i: (i // ns, 0, i % ns)
    tab = lambda rows: pl.BlockSpec((rows, TM), lambda i: (0, i % ns))
    out_shape = (
        jax.ShapeDtypeStruct((batch, A_QK, seq), BF16),
        jax.ShapeDtypeStruct((batch, A_HEADS, seq, 128), BF16),
        jax.ShapeDtypeStruct((batch, A_V, seq), BF16),
        jax.ShapeDtypeStruct((batch, B_Q, seq), BF16),
        jax.ShapeDtypeStruct((batch, seq, B_KV), BF16),
        jax.ShapeDtypeStruct((batch, B_KV, seq), BF16),
    )
    return pl.pallas_call(
        _proj_even_kernel,
        grid=(t // TM,),
        in_specs=[pl.BlockSpec((TM, D_MODEL), lambda i: (i, 0)),
                  _resident((1, D_MODEL)),
                  _resident((EVEN_IN, D_MODEL)),
                  tab(8), tab(8), tab(16), tab(16), tab(16), tab(16),
                  _resident((HEAD_DIM, 1)), _resident((HEAD_DIM, 1))],
        out_specs=(pl.BlockSpec((1, A_QK, TM), tok),
                   pl.BlockSpec((1, A_HEADS, TM, 128), lambda i: (i // ns, 0, i % ns, 0)),
                   pl.BlockSpec((1, A_V, TM), tok),
                   pl.BlockSpec((1, B_Q, TM), tok),
                   pl.BlockSpec((1, TM, B_KV), lambda i: (i // ns, i % ns, 0)),
                   pl.BlockSpec((1, B_KV, TM), tok)),
        out_shape=out_shape,
        compiler_params=_params(1),
        name="proj_even",
    )(xf, g, wt, *tabs, qn, kn)


def _softmax_step(s_t, vt, m_scr, l_scr, acc_scr):
    m_old = m_scr[...]
    m_new = jnp.maximum(m_old, jnp.max(s_t, axis=0, keepdims=True))
    alpha = jnp.exp(m_old - m_new)
    p = jnp.exp(s_t - m_new)
    l_scr[...] = alpha * l_scr[...] + jnp.sum(p, axis=0, keepdims=True)
    acc_scr[...] = alpha * acc_scr[...] + _dot(vt, p.astype(BF16))
    m_scr[...] = m_new


def _init_stats(m_scr, l_scr, acc_scr):
    m_scr[...] = jnp.full(m_scr.shape, -jnp.inf, F32)
    l_scr[...] = jnp.zeros(l_scr.shape, F32)
    acc_scr[...] = jnp.zeros(acc_scr.shape, F32)


def _attn_a_kernel(q_ref, k_ref, v_ref, lq1_ref, lk1_ref, lq2_ref, lk2_ref, sub_ref, o_ref,
                   m_scr, l_scr, acc_scr, *, lam_init, seq):
    qt = q_ref[0].astype(F32)
    row = lax.broadcasted_iota(jnp.int32, qt.shape, 0)
    zero = jnp.zeros_like(qt)
    w = jnp.concatenate([jnp.where(row < HEAD_DIM, qt, zero), jnp.where(row >= HEAD_DIM, qt, zero)],
                        axis=1).astype(BF16)
    _init_stats(m_scr, l_scr, acc_scr)

    def body(j, carry):
        off = pl.multiple_of(j * TK, TK)
        s_t = _dot(k_ref[0, 0, pl.ds(off, TK), :], w)
        _softmax_step(s_t, v_ref[0, :, pl.ds(off, TK)], m_scr, l_scr, acc_scr)
        return carry

    lax.fori_loop(0, seq // TK, body, 0)

    o = acc_scr[...] / l_scr[...]
    lam = (jnp.exp(jnp.sum(lq1_ref[...] * lk1_ref[...], axis=-1, keepdims=True))
           - jnp.exp(jnp.sum(lq2_ref[...] * lk2_ref[...], axis=-1, keepdims=True)) + lam_init)
    d = o[:, :TQ_A] - lam * o[:, TQ_A:]
    y = d * lax.rsqrt(jnp.mean(d * d, axis=0, keepdims=True) + EPS) * sub_ref[...]
    o_ref[0] = (y * (1.0 - lam_init)).T.astype(BF16)


def _attn_a(qta, ka, vta, lq1, lk1, lq2, lk2, subln, lam_init):
    batch, _, seq = qta.shape
    vec = _resident((1, HEAD_DIM))
    return pl.pallas_call(
        functools.partial(_attn_a_kernel, lam_init=lam_init, seq=seq),
        grid=(batch, A_HEADS, seq // TQ_A),
        in_specs=[pl.BlockSpec((1, 128, TQ_A), lambda b, h, q: (b, h, q)),
                  pl.BlockSpec((1, 1, seq, 128), lambda b, h, q: (b, h, 0, 0)),
                  pl.BlockSpec((1, 128, seq), lambda b, h, q: (b, h, 0)),
                  vec, vec, vec, vec,
                  _resident((128, 1))],
        out_specs=pl.BlockSpec((1, TQ_A, 128), lambda b, h, q: (b, q, h)),
        out_shape=jax.ShapeDtypeStruct((batch, seq, A_V), BF16),
        scratch_shapes=[pltpu.VMEM((1, 2 * TQ_A), F32), pltpu.VMEM((1, 2 * TQ_A), F32),
                        pltpu.VMEM((128, 2 * TQ_A), F32)],
        compiler_params=_params(3),
        name="attn_diff",
    )(qta, ka, vta, lq1, lk1, lq2, lk2, subln)


def _attn_b_kernel(q_ref, k_ref, v_ref, o_ref, m_scr, l_scr, acc_scr, *, seq):
    g = pl.program_id(1)
    rep = B_Q_HEADS // B_KV_HEADS
    qt = q_ref[0].astype(F32)
    wq = jnp.concatenate([qt[r * 64:(r + 1) * 64] for r in range(rep)], axis=1)
    zero = jnp.zeros_like(wq)
    w = jnp.concatenate([jnp.where(g == gg, wq, zero) for gg in range(B_KV_HEADS)], axis=0).astype(BF16)
    _init_stats(m_scr, l_scr, acc_scr)

    def body(j, carry):
        off = pl.multiple_of(j * TK, TK)
        s_t = _dot(k_ref[0, pl.ds(off, TK), :], w)
        _softmax_step(s_t, v_ref[0, :, pl.ds(off, TK)], m_scr, l_scr, acc_scr)
        return carry

    lax.fori_loop(0, seq // TK, body, 0)

    o = acc_scr[...] / l_scr[...]
    ot = jnp.concatenate([o[:, r * TQ_B:(r + 1) * TQ_B] for r in range(rep)], axis=0)
    o_ref[0] = ot.T.astype(BF16)


def _attn_b(qtb, kb, vtb):
    batch, _, seq = qtb.shape
    rep = B_Q_HEADS // B_KV_HEADS
    n = rep * TQ_B
    return pl.pallas_call(
        functools.partial(_attn_b_kernel, seq=seq),
        grid=(batch, B_KV_HEADS, seq // TQ_B),
        in_specs=[pl.BlockSpec((1, rep * HEAD_DIM, TQ_B), lambda b, g, q: (b, g, q)),
                  pl.BlockSpec((1, seq, B_KV), lambda b, g, q: (b, 0, 0)),
                  pl.BlockSpec((1, HEAD_DIM, seq), lambda b, g, q: (b, g, 0))],
        out_specs=pl.BlockSpec((1, TQ_B, rep * HEAD_DIM), lambda b, g, q: (b, q, g)),
        out_shape=jax.ShapeDtypeStruct((batch, seq, B_Q), BF16),
        scratch_shapes=[pltpu.VMEM((1, n), F32), pltpu.VMEM((1, n), F32), pltpu.VMEM((HEAD_DIM, n), F32)],
        compiler_params=_params(3),
        name="attn_gqa",
    )(qtb, kb, vtb)


def _proj_plain_kernel(x_ref, g_ref, w_ref, o_ref):
    h = _rms(x_ref[...], g_ref[...]).astype(BF16)
    o_ref[...] = _dot(h, w_ref[...]).astype(BF16)


def _proj_plain(xf, g, w):
    t, n = xf.shape[0], w.shape[1]
    return pl.pallas_call(
        _proj_plain_kernel,
        grid=(t // TM,),
        in_specs=[pl.BlockSpec((TM, D_MODEL), lambda i: (i, 0)), _resident((1, D_MODEL)),
                  _resident((D_MODEL, n))],
        out_specs=pl.BlockSpec((TM, n), lambda i: (i, 0)),
        out_shape=jax.ShapeDtypeStruct((t, n), BF16),
        compiler_params=_params(1),
        name="proj_odd",
    )(xf, g, w)


def _na_bias_kernel(tab_ref, o_ref):
    h = pl.program_id(0)
    shp = (GRID_W, 2 * GRID_W)
    kc = lax.broadcasted_iota(jnp.int32, shp, 0)
    lane = lax.broadcasted_iota(jnp.int32, shp, 1)
    qc = lane & (GRID_W - 1)
    cs = jnp.clip(qc - NA_KW // 2, 0, GRID_W - NA_KW)
    col_valid = (kc >= cs) & (kc < cs + NA_KW)
    dci = jnp.clip(kc - qc + NA_KW - 1, 0, 2 * NA_KW - 2)
    neg = jnp.full(shp, -jnp.inf, F32)
    n_dc = 2 * NA_KW - 1
    n_dr = 2 * NA_KH - 1
    planes = []
    for dr in range(n_dr):
        acc = jnp.zeros(shp, F32)
        for dd in range(n_dc):
            acc = jnp.where(dci == dd, tab_ref[h, dr * n_dc + dd], acc)
        planes.append(jnp.where(col_valid, acc, neg))
    for i in range(NA_PT):
        dl, dr_ = i - NA_PT_OFF, i - NA_PT_OFF - 1
        left = planes[dl] if 0 <= dl < n_dr else neg
        right = planes[dr_] if 0 <= dr_ < n_dr else neg
        o_ref[0, i] = jnp.where(lane < GRID_W, left, right)


def _na_bias(rel_bias):
    tab = rel_bias.reshape(C_HEADS, -1)
    return pl.pallas_call(
        _na_bias_kernel,
        grid=(C_HEADS,),
        in_specs=[pl.BlockSpec(memory_space=pltpu.SMEM)],
        out_specs=pl.BlockSpec((1, NA_PT, GRID_W, 2 * GRID_W), lambda h: (h, 0, 0, 0)),
        out_shape=jax.ShapeDtypeStruct((C_HEADS, NA_PT, GRID_W, 2 * GRID_W), F32),
        compiler_params=_params(1),
        name="na_bias_table",
    )(tab)


def _na_kernel(q_ref, k_ref, v_ref, pt_ref, o_ref, s_scr, *, rows):
    r0 = pl.program_id(2) * NA_ROWS
    ws = jnp.clip(r0 - NA_KH // 2, 0, rows - NA_WIN)
    start = pl.multiple_of(ws * GRID_W, GRID_W)
    nk = NA_WIN * GRID_W
    nq = NA_ROWS * GRID_W
    kwin = k_ref[0, pl.ds(start, nk), :]
    vt = v_ref[0, pl.ds(start, nk), :].astype(F32).T.astype(BF16)
    qt = (q_ref[0].astype(F32) * SCALE).T
    hrow = lax.broadcasted_iota(jnp.int32, qt.shape, 0) // HEAD_DIM
    kr = ws + lax.broadcasted_iota(jnp.int32, (nk, nq), 0) // GRID_W
    r = r0 + lax.broadcasted_iota(jnp.int32, (nk, nq), 1) // GRID_W
    rs = jnp.clip(r - NA_KH // 2, 0, rows - NA_KH)
    row_mask = jnp.where((kr >= rs) & (kr < rs + NA_KH), 0.0, -jnp.inf).astype(F32)
    outs = []
    for hh in range(2):
        w = jnp.where(hrow == hh, qt, jnp.zeros_like(qt)).astype(BF16)
        s_t = _dot(kwin, w)
        for j in range(NA_WIN):
            for a0 in range(0, NA_ROWS, 2):
                idx = ws - r0 + (j - a0 + NA_KH - 1 + NA_PT_OFF)
                rsl, csl = slice(j * GRID_W, (j + 1) * GRID_W), slice(a0 * GRID_W, (a0 + 2) * GRID_W)
                s_scr[rsl, csl] = s_t[rsl, csl] + pt_ref[hh, idx]
        s = s_scr[...] + row_mask
        p = jnp.exp(s - jnp.max(s, axis=0, keepdims=True))
        l = jnp.sum(p, axis=0, keepdims=True)
        ot = _dot(vt, p.astype(BF16))
        outs.append(ot[hh * HEAD_DIM:(hh + 1) * HEAD_DIM] / l)
    o_ref[0] = jnp.concatenate(outs, axis=0).T.astype(BF16)


def _na(qkv, pt):
    batch, seq, _ = qkv.shape
    rows = seq // GRID_W
    nq = NA_ROWS * GRID_W
    pairs = C_HEADS // 2
    return pl.pallas_call(
        functools.partial(_na_kernel, rows=rows),
        grid=(batch, pairs, rows // NA_ROWS),
        in_specs=[pl.BlockSpec((1, nq, 128), lambda b, h, r: (b, r, h)),
                  pl.BlockSpec((1, seq, 128), lambda b, h, r: (b, 0, pairs + h)),
                  pl.BlockSpec((1, seq, 128), lambda b, h, r: (b, 0, 2 * pairs + h)),
                  pl.BlockSpec((2, NA_PT, GRID_W, 2 * GRID_W), lambda b, h, r: (h, 0, 0, 0))],
        out_specs=pl.BlockSpec((1, nq, 128), lambda b, h, r: (b, r, h)),
        out_shape=jax.ShapeDtypeStruct((batch, seq, C_MIX), BF16),
        scratch_shapes=[pltpu.VMEM((NA_WIN * GRID_W, nq), F32)],
        compiler_params=_params(3),
        name="attn_nbr",
    )(qkv, qkv, qkv, pt)


def _outproj_kernel(*refs):
    n = (len(refs) - 2) // 2
    x_ref, o_ref = refs[0], refs[-1]
    y = x_ref[...]
    for a_ref, w_ref in zip(refs[1:1 + n], refs[1 + n:1 + 2 * n]):
        y = y + _dot(a_ref[...], w_ref[...])
    o_ref[...] = y


def _outproj(xf, mixes, weights):
    t = xf.shape[0]
    tile = lambda c: pl.BlockSpec((TM, c), lambda i: (i, 0))
    return pl.pallas_call(
        _outproj_kernel,
        grid=(t // TM,),
        in_specs=[tile(D_MODEL)] + [tile(m.shape[1]) for m in mixes] + [_resident(w.shape) for w in weights],
        out_specs=tile(D_MODEL),
        out_shape=jax.ShapeDtypeStruct((t, D_MODEL), F32),
        compiler_params=_params(1),
        name="out_proj",
    )(xf, *mixes, *weights)


def _ffn_kernel(xm_ref, xp_ref, xn_ref, g_ref, wu_ref, cw_ref, cb_ref, wd_ref, pg_ref, wg_ref, p_ref, wp_ref,
                fg_ref, o_ref, h_scr, acc_scr, *, tiles_per_seq, final):
    i = pl.program_id(0)
    g = g_ref[...]
    xm = xm_ref[...]
    pos = i % tiles_per_seq
    hp = jnp.where(pos == 0, 0.0, _rms(xp_ref[...], g))
    hn = jnp.where(pos == tiles_per_seq - 1, 0.0, _rms(xn_ref[...], g))
    h_scr[...] = jnp.concatenate([hp, _rms(xm, g), hn], axis=0).astype(BF16)
    acc_scr[...] = jnp.zeros(acc_scr.shape, F32)

    def conv(u, col):
        cw = cw_ref[:, pl.ds(col, TF)]
        return (u[HALO - 1:HALO - 1 + TM] * cw[0:1] + u[HALO:HALO + TM] * cw[1:2]
                + u[HALO + 1:HALO + 1 + TM] * cw[2:3] + cb_ref[:, pl.ds(col, TF)])

    def chunk(c, carry):
        off = pl.multiple_of(c * TF, TF)
        h = h_scr[...]
        gate = conv(_dot(h, wu_ref[:, pl.ds(off, TF)]), off)
        val = conv(_dot(h, wu_ref[:, pl.ds(D_FF + off, TF)]), D_FF + off)
        act = gate * jax.nn.sigmoid(gate) * val
        acc_scr[...] += _dot(act.astype(BF16), wd_ref[pl.ds(off, TF), :])
        return carry

    lax.fori_loop(0, D_FF // TF, chunk, 0)

    x2 = xm + acc_scr[...]
    gate = jax.nn.sigmoid(_dot(_rms(x2, pg_ref[...]).astype(BF16), wg_ref[...]))
    x3 = x2 + gate * _dot(p_ref[...].astype(BF16), wp_ref[...])
    if final:
        x3 = _rms(x3, fg_ref[...])
    o_ref[...] = x3


def _ffn(xf, g, wu, cw, cb, wd, pg, wg, p, wp, fg, seq, final):
    t = xf.shape[0]
    tps = seq // TM
    nb = TM // HALO
    last = t // HALO - 1
    return pl.pallas_call(
        functools.partial(_ffn_kernel, tiles_per_seq=tps, final=final),
        grid=(t // TM,),
        in_specs=[pl.BlockSpec((TM, D_MODEL), lambda i: (i, 0)),
                  pl.BlockSpec((HALO, D_MODEL), lambda i: (jnp.maximum(i * nb - 1, 0), 0)),
                  pl.BlockSpec((HALO, D_MODEL), lambda i: (jnp.minimum((i + 1) * nb, last), 0)),
                  _resident((1, D_MODEL)),
                  _resident((D_MODEL, 2 * D_FF)),
                  _resident((3, 2 * D_FF)), _resident((1, 2 * D_FF)),
                  _resident((D_FF, D_MODEL)),
                  _resident((1, D_MODEL)),
                  _resident((D_MODEL, D_MODEL)),
                  pl.BlockSpec((TM, PLE_DIM), lambda i: (i, 0)),
                  _resident((PLE_DIM, D_MODEL)),
                  _resident((1, D_MODEL))],
        out_specs=pl.BlockSpec((TM, D_MODEL), lambda i: (i, 0)),
        out_shape=jax.ShapeDtypeStruct((t, D_MODEL), F32),
        scratch_shapes=[pltpu.VMEM((TM + 2 * HALO, D_MODEL), BF16), pltpu.VMEM((TM, D_MODEL), F32)],
        compiler_params=_params(1),
        name="convffn_ple",
    )(xf, xf, xf, g, wu, cw, cb, wd, pg, wg, p, wp, fg)


def _rope_tables(seq):
    pos = jnp.arange(seq, dtype=jnp.int32)

    def tab(p, half, theta):
        inv = theta ** (-jnp.arange(half, dtype=F32) / half)
        ang = inv[:, None] * p.astype(F32)[None, :]
        return jnp.cos(ang), jnp.sin(ang)

    ca, sa = tab(pos, HEAD_DIM // 8, ROPE_THETA)
    cr, sr = tab(pos // GRID_W, HEAD_DIM // 4, AXIAL_THETA)
    cc, sc = tab(pos % GRID_W, HEAD_DIM // 4, AXIAL_THETA)
    return ca, sa, cr, sr, cc, sc


def kernel(x, p, attn_norm, w_in_ab, lambda_q1, lambda_k1, lambda_q2, lambda_k2, a_subln, b_q_norm, b_k_norm,
           w_out_ab, w_in_c, c_rel_bias, w_out_c, ffn_norm, w_ffn_up, ffn_conv_w, ffn_conv_b, w_ffn_down,
           ple_norm, w_ple_gate, w_ple_proj, final_norm):
    batch, seq, _ = x.shape
    depth = attn_norm.shape[0]
    t = batch * seq
    xf = x.reshape(t, D_MODEL)
    tabs = _rope_tables(seq)
    row = lambda v: v.reshape(1, -1)
    colv = lambda v: v.reshape(-1, 1)
    for i in range(depth):
        j = i // 2
        if i % 2 == 0:
            lam_init = 0.8 - 0.6 * math.exp(-0.3 * i)
            wt = w_in_ab[j].T.astype(BF16)
            qta, ka, vta, qtb, kb, vtb = _proj_even(xf, row(attn_norm[i]), wt, tabs, colv(b_q_norm[j]),
                                                    colv(b_k_norm[j]), batch, seq)
            mix_a = _attn_a(qta, ka, vta, row(lambda_q1[j]), row(lambda_k1[j]), row(lambda_q2[j]),
                            row(lambda_k2[j]), colv(a_subln[j]), lam_init)
            mix_b = _attn_b(qtb, kb, vtb)
            w_out = w_out_ab[j].astype(BF16)
            x1 = _outproj(xf, [mix_a.reshape(t, A_V), mix_b.reshape(t, B_Q)], [w_out[:A_V], w_out[A_V:]])
        else:
            qkv = _proj_plain(xf, row(attn_norm[i]), w_in_c[j].astype(BF16))
            o = _na(qkv.reshape(batch, seq, 3 * C_MIX), _na_bias(c_rel_bias[j]))
            x1 = _outproj(xf, [o.reshape(t, C_MIX)], [w_out_c[j].astype(BF16)])
        xf = _ffn(x1, row(ffn_norm[i]), w_ffn_up[i].astype(BF16), ffn_conv_w[i], row(ffn_conv_b[i]),
                  w_ffn_down[i].astype(BF16), row(ple_norm[i]), w_ple_gate[i].astype(BF16),
                  p[i].reshape(t, PLE_DIM), w_ple_proj[i].astype(BF16), row(final_norm), seq,
                  final=(i == depth - 1))
    return xf.reshape(batch, seq, D_MODEL)
```

```python
import functools
import math

import jax
import jax.numpy as jnp
from jax import lax
from jax.experimental import pallas as pl
from jax.experimental.pallas import tpu as pltpu

F32 = jnp.float32
BF16 = jnp.bfloat16

D_MODEL = 1024
HEAD_DIM = 64
A_HEADS = 4
B_Q_HEADS = 8
B_KV_HEADS = 2
C_HEADS = 16
ROPE_THETA = 500000.0
AXIAL_THETA = 10000.0
GRID_W = 64
NA_KH = 8
NA_KW = 16
D_FF = 2816
PLE_DIM = 256
EPS = 1e-6
SCALE = HEAD_DIM ** -0.5
LOG2E = math.log2(math.e)
Q_SCALE = SCALE * LOG2E

A_QK = A_HEADS * 2 * HEAD_DIM
A_V = A_HEADS * 2 * HEAD_DIM
B_Q = B_Q_HEADS * HEAD_DIM
B_KV = B_KV_HEADS * HEAD_DIM
EVEN_IN = 2 * A_QK + A_V + B_Q + 2 * B_KV
C_MIX = C_HEADS * HEAD_DIM

TM = 512
TQ_A = 256
TQ_B = 128
TK = 512
VB_ROWS = HEAD_DIM + 16
BLOCKS_PER_TRIP = 8
TF = 256
HALO = 8
NA_ROWS = 4
NA_WIN = NA_ROWS + NA_KH - 1
NA_PT_OFF = NA_ROWS - 2
NA_PT = NA_WIN + NA_KH - 1 + NA_PT_OFF
VMEM_LIMIT = 52 * 1024 * 1024

_NT = (((1,), (1,)), ((), ()))


def _dot(a, b):
    return jnp.dot(a, b, preferred_element_type=F32)


def _rms(xf, g):
    return xf * lax.rsqrt(jnp.mean(xf * xf, axis=-1, keepdims=True) + EPS) * g


def _params(n_axes, limit=VMEM_LIMIT):
    return pltpu.CompilerParams(dimension_semantics=("arbitrary",) * n_axes, vmem_limit_bytes=limit)


def _resident(shape):
    nd = len(shape)
    return pl.BlockSpec(shape, lambda *_: (0,) * nd, pipeline_mode=pl.Buffered(1))


def _proj_even_kernel(x_ref, g_ref, wt_ref, ca_ref, sa_ref, cr_ref, sr_ref, cc_ref, sc_ref, qn_ref, kn_ref,
                      qta_ref, ka_ref, vta_ref, qtb_ref, kb_ref, vtb_ref):
    h = _rms(x_ref[...], g_ref[...]).astype(BF16)

    def proj_t(lo, hi):
        return lax.dot_general(wt_ref[lo:hi, :], h, _NT, preferred_element_type=F32)

    ca, sa = ca_ref[...], sa_ref[...]
    cr, sr = cr_ref[...], sr_ref[...]
    cc, sc = cc_ref[...], sc_ref[...]

    def rope_a(blk):
        x1, x2 = blk[0:8], blk[8:16]
        return jnp.concatenate([x1 * ca - x2 * sa, x2 * ca + x1 * sa, blk[16:]], axis=0)

    def norm_rope_b(blk, gain):
        n = blk * lax.rsqrt(jnp.mean(blk * blk, axis=0, keepdims=True) + EPS) * gain
        r1, r2, c1, c2 = n[0:16], n[16:32], n[32:48], n[48:64]
        return jnp.concatenate([r1 * cr - r2 * sr, r2 * cr + r1 * sr,
                                c1 * cc - c2 * sc, c2 * cc + c1 * sc], axis=0)

    o = 0
    aq = proj_t(o, o + A_QK)
    qta_ref[0] = (jnp.concatenate([rope_a(aq[m * 64:(m + 1) * 64]) for m in range(2 * A_HEADS)], axis=0)
                  * Q_SCALE).astype(BF16)
    o += A_QK
    ak = proj_t(o, o + A_QK)
    for hh in range(A_HEADS):
        kt = jnp.concatenate([rope_a(ak[(2 * hh + m) * 64:(2 * hh + m + 1) * 64]) for m in range(2)], axis=0)
        ka_ref[0, hh] = kt.T.astype(BF16)
    o += A_QK
    vta_ref[0] = proj_t(o, o + A_V).astype(BF16)
    o += A_V
    bq = proj_t(o, o + B_Q)
    qn = qn_ref[...]
    qtb_ref[0] = (jnp.concatenate([norm_rope_b(bq[m * 64:(m + 1) * 64], qn) for m in range(B_Q_HEADS)], axis=0)
                  * Q_SCALE).astype(BF16)
    o += B_Q
    bk = proj_t(o, o + B_KV)
    kn = kn_ref[...]
    kbt = jnp.concatenate([norm_rope_b(bk[m * 64:(m + 1) * 64], kn) for m in range(B_KV_HEADS)], axis=0)
    kb_ref[0] = kbt.T.astype(BF16)
    o += B_KV
    bv = proj_t(o, o + B_KV)
    pad = jnp.concatenate([jnp.ones((8, TM), F32), jnp.zeros((VB_ROWS - HEAD_DIM - 8, TM), F32)], axis=0)
    for gg in range(B_KV_HEADS):
        vtb_ref[0, gg] = jnp.concatenate([bv[gg * 64:(gg + 1) * 64], pad], axis=0).astype(BF16)


def _proj_even(xf, g, wt, tabs, qn, kn, batch, seq):
    t = xf.shape[0]
    ns = seq // TM
    tok = lambda i: (i // ns, 0, i % ns)
    tab = lambda rows: pl.BlockSpec((rows, TM), lambda i: (0, i % ns))
    out_shape = (
        jax.ShapeDtypeStruct((batch, A_QK, seq), BF16),
        jax.ShapeDtypeStruct((batch, A_HEADS, seq, 128), BF16),
        jax.ShapeDtypeStruct((batch, A_V, seq), BF16),
        jax.ShapeDtypeStruct((batch, B_Q, seq), BF16),
        jax.ShapeDtypeStruct((batch, seq, B_KV), BF16),
        jax.ShapeDtypeStruct((batch, B_KV_HEADS, VB_ROWS, seq), BF16),
    )
    return pl.pallas_call(
        _proj_even_kernel,
        grid=(t // TM,),
        in_specs=[pl.BlockSpec((TM, D_MODEL), lambda i: (i, 0)),
                  _resident((1, D_MODEL)),
                  _resident((EVEN_IN, D_MODEL)),
                  tab(8), tab(8), tab(16), tab(16), tab(16), tab(16),
                  _resident((HEAD_DIM, 1)), _resident((HEAD_DIM, 1))],
        out_specs=(pl.BlockSpec((1, A_QK, TM), tok),
                   pl.BlockSpec((1, A_HEADS, TM, 128), lambda i: (i // ns, 0, i % ns, 0)),
                   pl.BlockSpec((1, A_V, TM), tok),
                   pl.BlockSpec((1, B_Q, TM), tok),
                   pl.BlockSpec((1, TM, B_KV), lambda i: (i // ns, i % ns, 0)),
                   pl.BlockSpec((1, B_KV_HEADS, VB_ROWS, TM), lambda i: (i // ns, 0, 0, i % ns))),
        out_shape=out_shape,
        compiler_params=_params(1),
        name="proj_even",
    )(xf, g, wt, *tabs, qn, kn)


def _flash_loop(scores_fn, values_fn, n_blocks, s_scr, m_scr, l_scr, acc_scr):
    m_scr[...] = jnp.full(m_scr.shape, -jnp.inf, F32)
    acc_scr[...] = jnp.zeros(acc_scr.shape, F32)
    if l_scr is not None:
        l_scr[...] = jnp.zeros(l_scr.shape, F32)

    def produce(j, slot):
        s_t = scores_fn(j)
        s_scr[slot] = s_t
        return jnp.max(s_t, axis=0, keepdims=True)

    def consume(j, slot, blk_max):
        m_old = m_scr[...]
        m_new = jnp.maximum(m_old, blk_max)
        alpha = jnp.exp2(m_old - m_new)
        p = jnp.exp2(s_scr[slot] - m_new)
        if l_scr is not None:
            l_scr[...] = alpha * l_scr[...] + jnp.sum(p, axis=0, keepdims=True)
        acc_scr[...] = alpha * acc_scr[...] + _dot(values_fn(j), p.astype(BF16))
        m_scr[...] = m_new

    per_trip = min(BLOCKS_PER_TRIP, n_blocks)
    assert per_trip % 2 == 0 and n_blocks % per_trip == 0

    def body(i, blk_max):
        j = per_trip * i
        for u in range(per_trip):
            nxt = produce(jnp.minimum(j + u + 1, n_blocks - 1), (u + 1) % 2)
            consume(j + u, u % 2, blk_max)
            blk_max = nxt
        return blk_max

    lax.fori_loop(0, n_blocks // per_trip, body, produce(0, 0))


def _attn_a_kernel(q_ref, k_ref, v_ref, lq1_ref, lk1_ref, lq2_ref, lk2_ref, sub_ref, o_ref,
                   s_scr, m_scr, l_scr, acc_scr, *, lam_init, seq):
    qt = q_ref[0].astype(F32)
    row = lax.broadcasted_iota(jnp.int32, qt.shape, 0)
    zero = jnp.zeros_like(qt)
    w = jnp.concatenate([jnp.where(row < HEAD_DIM, qt, zero), jnp.where(row >= HEAD_DIM, qt, zero)],
                        axis=1).astype(BF16)
    blk = lambda j: pl.ds(pl.multiple_of(j * TK, TK), TK)
    _flash_loop(lambda j: _dot(k_ref[0, 0, blk(j), :], w),
                lambda j: v_ref[0, :, blk(j)], seq // TK, s_scr, m_scr, l_scr, acc_scr)

    o = acc_scr[...] / l_scr[...]
    lam = (jnp.exp(jnp.sum(lq1_ref[...] * lk1_ref[...], axis=-1, keepdims=True))
           - jnp.exp(jnp.sum(lq2_ref[...] * lk2_ref[...], axis=-1, keepdims=True)) + lam_init)
    d = o[:, :TQ_A] - lam * o[:, TQ_A:]
    y = d * lax.rsqrt(jnp.mean(d * d, axis=0, keepdims=True) + EPS) * sub_ref[...]
    o_ref[0] = (y * (1.0 - lam_init)).T.astype(BF16)


def _attn_a(qta, ka, vta, lq1, lk1, lq2, lk2, subln, lam_init):
    batch, _, seq = qta.shape
    vec = _resident((1, HEAD_DIM))
    return pl.pallas_call(
        functools.partial(_attn_a_kernel, lam_init=lam_init, seq=seq),
        grid=(batch, A_HEADS, seq // TQ_A),
        in_specs=[pl.BlockSpec((1, 128, TQ_A), lambda b, h, q: (b, h, q)),
                  pl.BlockSpec((1, 1, seq, 128), lambda b, h, q: (b, h, 0, 0)),
                  pl.BlockSpec((1, 128, seq), lambda b, h, q: (b, h, 0)),
                  vec, vec, vec, vec,
                  _resident((128, 1))],
        out_specs=pl.BlockSpec((1, TQ_A, 128), lambda b, h, q: (b, q, h)),
        out_shape=jax.ShapeDtypeStruct((batch, seq, A_V), BF16),
        scratch_shapes=[pltpu.VMEM((2, TK, 2 * TQ_A), F32), pltpu.VMEM((1, 2 * TQ_A), F32),
                        pltpu.VMEM((1, 2 * TQ_A), F32), pltpu.VMEM((128, 2 * TQ_A), F32)],
        compiler_params=_params(3),
        name="attn_diff",
    )(qta, ka, vta, lq1, lk1, lq2, lk2, subln)


def _attn_b_kernel(q_ref, k_ref, v_ref, o_ref, s_scr, m_scr, acc_scr, *, seq):
    g = pl.program_id(1)
    rep = B_Q_HEADS // B_KV_HEADS
    qt = q_ref[0].astype(F32)
    wq = jnp.concatenate([qt[r * 64:(r + 1) * 64] for r in range(rep)], axis=1)
    zero = jnp.zeros_like(wq)
    w = jnp.concatenate([jnp.where(g == gg, wq, zero) for gg in range(B_KV_HEADS)], axis=0).astype(BF16)
    blk = lambda j: pl.ds(pl.multiple_of(j * TK, TK), TK)
    _flash_loop(lambda j: _dot(k_ref[0, blk(j), :], w),
                lambda j: v_ref[0, 0, :, blk(j)], seq // TK, s_scr, m_scr, None, acc_scr)

    o = acc_scr[0:HEAD_DIM, :] / acc_scr[HEAD_DIM:HEAD_DIM + 1, :]
    ot = jnp.concatenate([o[:, r * TQ_B:(r + 1) * TQ_B] for r in range(rep)], axis=0)
    o_ref[0] = ot.T.astype(BF16)


def _attn_b(qtb, kb, vtb):
    batch, _, seq = qtb.shape
    rep = B_Q_HEADS // B_KV_HEADS
    n = rep * TQ_B
    return pl.pallas_call(
        functools.partial(_attn_b_kernel, seq=seq),
        grid=(batch, B_KV_HEADS, seq // TQ_B),
        in_specs=[pl.BlockSpec((1, rep * HEAD_DIM, TQ_B), lambda b, g, q: (b, g, q)),
                  pl.BlockSpec((1, seq, B_KV), lambda b, g, q: (b, 0, 0)),
                  pl.BlockSpec((1, 1, VB_ROWS, seq), lambda b, g, q: (b, g, 0, 0))],
        out_specs=pl.BlockSpec((1, TQ_B, rep * HEAD_DIM), lambda b, g, q: (b, q, g)),
        out_shape=jax.ShapeDtypeStruct((batch, seq, B_Q), BF16),
        scratch_shapes=[pltpu.VMEM((2, TK, n), F32), pltpu.VMEM((1, n), F32), pltpu.VMEM((VB_ROWS, n), F32)],
        compiler_params=_params(3),
        name="attn_gqa",
    )(qtb, kb, vtb)


def _proj_plain_kernel(x_ref, g_ref, w_ref, o_ref):
    h = _rms(x_ref[...], g_ref[...]).astype(BF16)
    o_ref[...] = _dot(h, w_ref[...]).astype(BF16)


def _proj_plain(xf, g, w):
    t, n = xf.shape[0], w.shape[1]
    return pl.pallas_call(
        _proj_plain_kernel,
        grid=(t // TM,),
        in_specs=[pl.BlockSpec((TM, D_MODEL), lambda i: (i, 0)), _resident((1, D_MODEL)),
                  _resident((D_MODEL, n))],
        out_specs=pl.BlockSpec((TM, n), lambda i: (i, 0)),
        out_shape=jax.ShapeDtypeStruct((t, n), BF16),
        compiler_params=_params(1),
        name="proj_odd",
    )(xf, g, w)


def _na_bias_kernel(tab_ref, o_ref):
    h = pl.program_id(0)
    shp = (GRID_W, 2 * GRID_W)
    kc = lax.broadcasted_iota(jnp.int32, shp, 0)
    lane = lax.broadcasted_iota(jnp.int32, shp, 1)
    qc = lane & (GRID_W - 1)
    cs = jnp.clip(qc - NA_KW // 2, 0, GRID_W - NA_KW)
    col_valid = (kc >= cs) & (kc < cs + NA_KW)
    dci = jnp.clip(kc - qc + NA_KW - 1, 0, 2 * NA_KW - 2)
    neg = jnp.full(shp, -jnp.inf, F32)
    n_dc = 2 * NA_KW - 1
    n_dr = 2 * NA_KH - 1
    planes = []
    for dr in range(n_dr):
        acc = jnp.zeros(shp, F32)
        for dd in range(n_dc):
            acc = jnp.where(dci == dd, tab_ref[h, dr * n_dc + dd], acc)
        planes.append(jnp.where(col_valid, acc * LOG2E, neg))
    for i in range(NA_PT):
        dl, dr_ = i - NA_PT_OFF, i - NA_PT_OFF - 1
        left = planes[dl] if 0 <= dl < n_dr else neg
        right = planes[dr_] if 0 <= dr_ < n_dr else neg
        o_ref[0, i] = jnp.where(lane < GRID_W, left, right)


def _na_bias(rel_bias):
    tab = rel_bias.reshape(C_HEADS, -1)
    return pl.pallas_call(
        _na_bias_kernel,
        grid=(C_HEADS,),
        in_specs=[pl.BlockSpec(memory_space=pltpu.SMEM)],
        out_specs=pl.BlockSpec((1, NA_PT, GRID_W, 2 * GRID_W), lambda h: (h, 0, 0, 0)),
        out_shape=jax.ShapeDtypeStruct((C_HEADS, NA_PT, GRID_W, 2 * GRID_W), F32),
        compiler_params=_params(1),
        name="na_bias_table",
    )(tab)


def _na_window_start(r0, rows):
    return jnp.clip(r0 - NA_KH // 2, 0, rows - NA_WIN)


def _na_mask_kernel(o_ref, *, rows):
    c = pl.program_id(0)
    r0 = jnp.where(c == 0, 0, jnp.where(c == 1, NA_ROWS, rows - NA_ROWS))
    ws = _na_window_start(r0, rows)
    shp = (NA_WIN * GRID_W, NA_ROWS * GRID_W)
    kr = ws + lax.broadcasted_iota(jnp.int32, shp, 0) // GRID_W
    r = r0 + lax.broadcasted_iota(jnp.int32, shp, 1) // GRID_W
    rs = jnp.clip(r - NA_KH // 2, 0, rows - NA_KH)
    o_ref[0] = jnp.where((kr >= rs) & (kr < rs + NA_KH), 0.0, -jnp.inf).astype(F32)


def _na_mask(rows):
    shp = (NA_WIN * GRID_W, NA_ROWS * GRID_W)
    return pl.pallas_call(
        functools.partial(_na_mask_kernel, rows=rows),
        grid=(3,),
        out_specs=pl.BlockSpec((1,) + shp, lambda c: (c, 0, 0)),
        out_shape=jax.ShapeDtypeStruct((3,) + shp, F32),
        compiler_params=_params(1),
        name="na_row_mask",
    )()


def _na_kernel(q_ref, k_ref, v_ref, pt_ref, mask_ref, o_ref, s_scr, *, rows):
    r0 = pl.program_id(2) * NA_ROWS
    ws = _na_window_start(r0, rows)
    start = pl.multiple_of(ws * GRID_W, GRID_W)
    nk = NA_WIN * GRID_W
    kwin = k_ref[0, pl.ds(start, nk), :]
    vt = v_ref[0, pl.ds(start, nk), :].astype(F32).T.astype(BF16)
    qt = (q_ref[0].astype(F32) * Q_SCALE).T
    hrow = lax.broadcasted_iota(jnp.int32, qt.shape, 0) // HEAD_DIM
    outs = []
    for hh in range(2):
        w = jnp.where(hrow == hh, qt, jnp.zeros_like(qt)).astype(BF16)
        s_t = _dot(kwin, w)
        for j in range(NA_WIN):
            for a0 in range(0, NA_ROWS, 2):
                idx = ws - r0 + (j - a0 + NA_KH - 1 + NA_PT_OFF)
                rsl, csl = slice(j * GRID_W, (j + 1) * GRID_W), slice(a0 * GRID_W, (a0 + 2) * GRID_W)
                s_scr[hh, rsl, csl] = s_t[rsl, csl] + pt_ref[hh, idx]
        s = s_scr[hh] + mask_ref[0]
        p = jnp.exp2(s - jnp.max(s, axis=0, keepdims=True))
        l = jnp.sum(p, axis=0, keepdims=True)
        ot = _dot(vt, p.astype(BF16))
        outs.append(ot[hh * HEAD_DIM:(hh + 1) * HEAD_DIM] / l)
    o_ref[0] = jnp.concatenate(outs, axis=0).T.astype(BF16)


def _na(qkv, pt, mask):
    batch, seq, _ = qkv.shape
    rows = seq // GRID_W
    nq = NA_ROWS * GRID_W
    pairs = C_HEADS // 2
    last = rows // NA_ROWS - 1
    return pl.pallas_call(
        functools.partial(_na_kernel, rows=rows),
        grid=(batch, pairs, rows // NA_ROWS),
        in_specs=[pl.BlockSpec((1, nq, 128), lambda b, h, r: (b, r, h)),
                  pl.BlockSpec((1, seq, 128), lambda b, h, r: (b, 0, pairs + h)),
                  pl.BlockSpec((1, seq, 128), lambda b, h, r: (b, 0, 2 * pairs + h)),
                  pl.BlockSpec((2, NA_PT, GRID_W, 2 * GRID_W), lambda b, h, r: (h, 0, 0, 0)),
                  pl.BlockSpec((1, NA_WIN * GRID_W, nq),
                               lambda b, h, r: (jnp.where(r == 0, 0, jnp.where(r == last, 2, 1)), 0, 0))],
        out_specs=pl.BlockSpec((1, nq, 128), lambda b, h, r: (b, r, h)),
        out_shape=jax.ShapeDtypeStruct((batch, seq, C_MIX), BF16),
        scratch_shapes=[pltpu.VMEM((2, NA_WIN * GRID_W, nq), F32)],
        compiler_params=_params(3),
        name="attn_nbr",
    )(qkv, qkv, qkv, pt, mask)


def _outproj_kernel(*refs):
    n = (len(refs) - 2) // 2
    x_ref, o_ref = refs[0], refs[-1]
    y = x_ref[...]
    for a_ref, w_ref in zip(refs[1:1 + n], refs[1 + n:1 + 2 * n]):
        y = y + _dot(a_ref[...], w_ref[...])
    o_ref[...] = y


def _outproj(xf, mixes, weights):
    t = xf.shape[0]
    tile = lambda c: pl.BlockSpec((TM, c), lambda i: (i, 0))
    return pl.pallas_call(
        _outproj_kernel,
        grid=(t // TM,),
        in_specs=[tile(D_MODEL)] + [tile(m.shape[1]) for m in mixes] + [_resident(w.shape) for w in weights],
        out_specs=tile(D_MODEL),
        out_shape=jax.ShapeDtypeStruct((t, D_MODEL), F32),
        compiler_params=_params(1),
        name="out_proj",
    )(xf, *mixes, *weights)


def _ffn_kernel(xm_ref, xp_ref, xn_ref, g_ref, wu_ref, cw_ref, cb_ref, wd_ref, pg_ref, wg_ref, p_ref, wp_ref,
                fg_ref, o_ref, h_scr, acc_scr, *, tiles_per_seq, final):
    i = pl.program_id(0)
    g = g_ref[...]
    xm = xm_ref[...]
    pos = i % tiles_per_seq
    hp = jnp.where(pos == 0, 0.0, _rms(xp_ref[...], g))
    hn = jnp.where(pos == tiles_per_seq - 1, 0.0, _rms(xn_ref[...], g))
    h_scr[...] = jnp.concatenate([hp, _rms(xm, g), hn], axis=0).astype(BF16)
    acc_scr[...] = jnp.zeros(acc_scr.shape, F32)

    def conv(u, col):
        cw = cw_ref[:, pl.ds(col, TF)]
        rows = u.shape[0]
        y = (pltpu.roll(u, 1, 0) * cw[0:1] + u * cw[1:2] + pltpu.roll(u, rows - 1, 0) * cw[2:3]
             + cb_ref[:, pl.ds(col, TF)])
        return y[HALO:HALO + TM]

    for c in range(D_FF // TF):
        off = c * TF
        h = h_scr[...]
        gate = conv(_dot(h, wu_ref[:, pl.ds(off, TF)]), off)
        val = conv(_dot(h, wu_ref[:, pl.ds(D_FF + off, TF)]), D_FF + off)
        act = gate * jax.nn.sigmoid(gate) * val
        acc_scr[...] += _dot(act.astype(BF16), wd_ref[pl.ds(off, TF), :])

    x2 = xm + acc_scr[...]
    gate = jax.nn.sigmoid(_dot(_rms(x2, pg_ref[...]).astype(BF16), wg_ref[...]))
    x3 = x2 + gate * _dot(p_ref[...].astype(BF16), wp_ref[...])
    if final:
        x3 = _rms(x3, fg_ref[...])
    o_ref[...] = x3


def _ffn(xf, g, wu, cw, cb, wd, pg, wg, p, wp, fg, seq, final):
    t = xf.shape[0]
    tps = seq // TM
    nb = TM // HALO
    last = t // HALO - 1
    return pl.pallas_call(
        functools.partial(_ffn_kernel, tiles_per_seq=tps, final=final),
        grid=(t // TM,),
        in_specs=[pl.BlockSpec((TM, D_MODEL), lambda i: (i, 0)),
                  pl.BlockSpec((HALO, D_MODEL), lambda i: (jnp.maximum(i * nb - 1, 0), 0)),
                  pl.BlockSpec((HALO, D_MODEL), lambda i: (jnp.minimum((i + 1) * nb, last), 0)),
                  _resident((1, D_MODEL)),
                  _resident((D_MODEL, 2 * D_FF)),
                  _resident((3, 2 * D_FF)), _resident((1, 2 * D_FF)),
                  _resident((D_FF, D_MODEL)),
                  _resident((1, D_MODEL)),
                  _resident((D_MODEL, D_MODEL)),
                  pl.BlockSpec((TM, PLE_DIM), lambda i: (i, 0)),
                  _resident((PLE_DIM, D_MODEL)),
                  _resident((1, D_MODEL))],
        out_specs=pl.BlockSpec((TM, D_MODEL), lambda i: (i, 0)),
        out_shape=jax.ShapeDtypeStruct((t, D_MODEL), F32),
        scratch_shapes=[pltpu.VMEM((TM + 2 * HALO, D_MODEL), BF16), pltpu.VMEM((TM, D_MODEL), F32)],
        compiler_params=_params(1),
        name="convffn_ple",
    )(xf, xf, xf, g, wu, cw, cb, wd, pg, wg, p, wp, fg)


def _rope_tables(seq):
    pos = jnp.arange(seq, dtype=jnp.int32)

    def tab(p, half, theta):
        inv = theta ** (-jnp.arange(half, dtype=F32) / half)
        ang = inv[:, None] * p.astype(F32)[None, :]
        return jnp.cos(ang), jnp.sin(ang)

    ca, sa = tab(pos, HEAD_DIM // 8, ROPE_THETA)
    cr, sr = tab(pos // GRID_W, HEAD_DIM // 4, AXIAL_THETA)
    cc, sc = tab(pos % GRID_W, HEAD_DIM // 4, AXIAL_THETA)
    return ca, sa, cr, sr, cc, sc


def kernel(x, p, attn_norm, w_in_ab, lambda_q1, lambda_k1, lambda_q2, lambda_k2, a_subln, b_q_norm, b_k_norm,
           w_out_ab, w_in_c, c_rel_bias, w_out_c, ffn_norm, w_ffn_up, ffn_conv_w, ffn_conv_b, w_ffn_down,
           ple_norm, w_ple_gate, w_ple_proj, final_norm):
    batch, seq, _ = x.shape
    depth = attn_norm.shape[0]
    t = batch * seq
    xf = x.reshape(t, D_MODEL)
    tabs = _rope_tables(seq)
    row = lambda v: v.reshape(1, -1)
    colv = lambda v: v.reshape(-1, 1)
    for i in range(depth):
        j = i // 2
        if i % 2 == 0:
            lam_init = 0.8 - 0.6 * math.exp(-0.3 * i)
            wt = w_in_ab[j].T.astype(BF16)
            qta, ka, vta, qtb, kb, vtb = _proj_even(xf, row(attn_norm[i]), wt, tabs, colv(b_q_norm[j]),
                                                    colv(b_k_norm[j]), batch, seq)
            mix_a = _attn_a(qta, ka, vta, row(lambda_q1[j]), row(lambda_k1[j]), row(lambda_q2[j]),
                            row(lambda_k2[j]), colv(a_subln[j]), lam_init)
            mix_b = _attn_b(qtb, kb, vtb)
            w_out = w_out_ab[j].astype(BF16)
            x1 = _outproj(xf, [mix_a.reshape(t, A_V), mix_b.reshape(t, B_Q)], [w_out[:A_V], w_out[A_V:]])
        else:
            qkv = _proj_plain(xf, row(attn_norm[i]), w_in_c[j].astype(BF16))
            o = _na(qkv.reshape(batch, seq, 3 * C_MIX), _na_bias(c_rel_bias[j]), _na_mask(seq // GRID_W))
            x1 = _outproj(xf, [o.reshape(t, C_MIX)], [w_out_c[j].astype(BF16)])
        xf = _ffn(x1, row(ffn_norm[i]), w_ffn_up[i].astype(BF16), ffn_conv_w[i], row(ffn_conv_b[i]),
                  w_ffn_down[i].astype(BF16), row(ple_norm[i]), w_ple_gate[i].astype(BF16),
                  p[i].reshape(t, PLE_DIM), w_ple_proj[i].astype(BF16), row(final_norm), seq,
                  final=(i == depth - 1))
    return xf.reshape(batch, seq, D_MODEL)
```

```python
import functools
import math

import jax
import jax.numpy as jnp
from jax import lax
from jax.experimental import pallas as pl
from jax.experimental.pallas import tpu as pltpu

F32 = jnp.float32
BF16 = jnp.bfloat16

D_MODEL = 1024
HEAD_DIM = 64
A_HEADS = 4
B_Q_HEADS = 8
B_KV_HEADS = 2
C_HEADS = 16
ROPE_THETA = 500000.0
AXIAL_THETA = 10000.0
GRID_W = 64
NA_KH = 8
NA_KW = 16
D_FF = 2816
PLE_DIM = 256
EPS = 1e-6
SCALE = HEAD_DIM ** -0.5
LOG2E = math.log2(math.e)
Q_SCALE = SCALE * LOG2E

A_QK = A_HEADS * 2 * HEAD_DIM
A_V = A_HEADS * 2 * HEAD_DIM
B_Q = B_Q_HEADS * HEAD_DIM
B_KV = B_KV_HEADS * HEAD_DIM
EVEN_IN = 2 * A_QK + A_V + B_Q + 2 * B_KV
C_MIX = C_HEADS * HEAD_DIM

TM = 512
TQ_A = 512
TQ_B = 256
TK = 512
VB_ROWS = HEAD_DIM + 16
BLOCKS_PER_TRIP = 8
TF = 256
HALO = 8
NA_ROWS = 4
NA_WIN = NA_ROWS + NA_KH
NA_PT_OFF = NA_WIN - NA_KH - 1
NA_PT = NA_WIN + NA_KH - 1 + NA_PT_OFF
NA_GROUPS_PER_TRIP = 4
VMEM_LIMIT = 52 * 1024 * 1024

_NT = (((1,), (1,)), ((), ()))


def _dot(a, b):
    return jnp.dot(a, b, preferred_element_type=F32)


def _rms(xf, g):
    return xf * lax.rsqrt(jnp.mean(xf * xf, axis=-1, keepdims=True) + EPS) * g


def _params(n_axes, limit=VMEM_LIMIT):
    return pltpu.CompilerParams(dimension_semantics=("arbitrary",) * n_axes, vmem_limit_bytes=limit)


def _resident(shape):
    nd = len(shape)
    return pl.BlockSpec(shape, lambda *_: (0,) * nd, pipeline_mode=pl.Buffered(1))


def _proj_even_kernel(x_ref, g_ref, wt_ref, ca_ref, sa_ref, cr_ref, sr_ref, cc_ref, sc_ref, qn_ref, kn_ref,
                      qta_ref, ka_ref, vta_ref, qtb_ref, kb_ref, vtb_ref):
    h = _rms(x_ref[...], g_ref[...]).astype(BF16)

    def proj_t(lo, hi):
        return lax.dot_general(wt_ref[lo:hi, :], h, _NT, preferred_element_type=F32)

    ca, sa = ca_ref[...], sa_ref[...]
    cr, sr = cr_ref[...], sr_ref[...]
    cc, sc = cc_ref[...], sc_ref[...]

    def rope_a(blk):
        x1, x2 = blk[0:8], blk[8:16]
        return jnp.concatenate([x1 * ca - x2 * sa, x2 * ca + x1 * sa, blk[16:]], axis=0)

    def norm_rope_b(blk, gain):
        n = blk * lax.rsqrt(jnp.mean(blk * blk, axis=0, keepdims=True) + EPS) * gain
        r1, r2, c1, c2 = n[0:16], n[16:32], n[32:48], n[48:64]
        return jnp.concatenate([r1 * cr - r2 * sr, r2 * cr + r1 * sr,
                                c1 * cc - c2 * sc, c2 * cc + c1 * sc], axis=0)

    o = 0
    aq = proj_t(o, o + A_QK)
    qta_ref[0] = (jnp.concatenate([rope_a(aq[m * 64:(m + 1) * 64]) for m in range(2 * A_HEADS)], axis=0)
                  * Q_SCALE).astype(BF16)
    o += A_QK
    ak = proj_t(o, o + A_QK)
    for hh in range(A_HEADS):
        kt = jnp.concatenate([rope_a(ak[(2 * hh + m) * 64:(2 * hh + m + 1) * 64]) for m in range(2)], axis=0)
        ka_ref[0, hh] = kt.T.astype(BF16)
    o += A_QK
    vta_ref[0] = proj_t(o, o + A_V).astype(BF16)
    o += A_V
    bq = proj_t(o, o + B_Q)
    qn = qn_ref[...]
    qtb_ref[0] = (jnp.concatenate([norm_rope_b(bq[m * 64:(m + 1) * 64], qn) for m in range(B_Q_HEADS)], axis=0)
                  * Q_SCALE).astype(BF16)
    o += B_Q
    bk = proj_t(o, o + B_KV)
    kn = kn_ref[...]
    kbt = jnp.concatenate([norm_rope_b(bk[m * 64:(m + 1) * 64], kn) for m in range(B_KV_HEADS)], axis=0)
    kb_ref[0] = kbt.T.astype(BF16)
    o += B_KV
    bv = proj_t(o, o + B_KV)
    pad = jnp.concatenate([jnp.ones((8, TM), F32), jnp.zeros((VB_ROWS - HEAD_DIM - 8, TM), F32)], axis=0)
    for gg in range(B_KV_HEADS):
        vtb_ref[0, gg] = jnp.concatenate([bv[gg * 64:(gg + 1) * 64], pad], axis=0).astype(BF16)


def _proj_even(xf, g, wt, tabs, qn, kn, batch, seq):
    t = xf.shape[0]
    ns = seq // TM
    tok = lambda i: (i // ns, 0, i % ns)
    tab = lambda rows: pl.BlockSpec((rows, TM), lambda i: (0, i % ns))
    out_shape = (
        jax.ShapeDtypeStruct((batch, A_QK, seq), BF16),
        jax.ShapeDtypeStruct((batch, A_HEADS, seq, 128), BF16),
        jax.ShapeDtypeStruct((batch, A_V, seq), BF16),
        jax.ShapeDtypeStruct((batch, B_Q, seq), BF16),
        jax.ShapeDtypeStruct((batch, seq, B_KV), BF16),
        jax.ShapeDtypeStruct((batch, B_KV_HEADS, VB_ROWS, seq), BF16),
    )
    return pl.pallas_call(
        _proj_even_kernel,
        grid=(t // TM,),
        in_specs=[pl.BlockSpec((TM, D_MODEL), lambda i: (i, 0)),
                  _resident((1, D_MODEL)),
                  _resident((EVEN_IN, D_MODEL)),
                  tab(8), tab(8), tab(16), tab(16), tab(16), tab(16),
                  _resident((HEAD_DIM, 1)), _resident((HEAD_DIM, 1))],
        out_specs=(pl.BlockSpec((1, A_QK, TM), tok),
                   pl.BlockSpec((1, A_HEADS, TM, 128), lambda i: (i // ns, 0, i % ns, 0)),
                   pl.BlockSpec((1, A_V, TM), tok),
                   pl.BlockSpec((1, B_Q, TM), tok),
                   pl.BlockSpec((1, TM, B_KV), lambda i: (i // ns, i % ns, 0)),
                   pl.BlockSpec((1, B_KV_HEADS, VB_ROWS, TM), lambda i: (i // ns, 0, 0, i % ns))),
        out_shape=out_shape,
        compiler_params=_params(1),
        name="proj_even",
    )(xf, g, wt, *tabs, qn, kn)


def _flash_loop(scores_fn, values_fn, n_blocks, s_scr, m_scr, l_scr, acc_scr):
    m_scr[...] = jnp.full(m_scr.shape, -jnp.inf, F32)
    acc_scr[...] = jnp.zeros(acc_scr.shape, F32)
    if l_scr is not None:
        l_scr[...] = jnp.zeros(l_scr.shape, F32)

    def produce(j, slot):
        s_t = scores_fn(j)
        s_scr[slot] = s_t
        return jnp.max(s_t, axis=0, keepdims=True)

    def consume(j, slot, blk_max):
        m_old = m_scr[...]
        m_new = jnp.maximum(m_old, blk_max)
        alpha = jnp.exp2(m_old - m_new)
        p = jnp.exp2(s_scr[slot] - m_new)
        if l_scr is not None:
            l_scr[...] = alpha * l_scr[...] + jnp.sum(p, axis=0, keepdims=True)
        acc_scr[...] = alpha * acc_scr[...] + _dot(values_fn(j), p.astype(BF16))
        m_scr[...] = m_new

    per_trip = min(BLOCKS_PER_TRIP, n_blocks)
    assert per_trip % 2 == 0 and n_blocks % per_trip == 0

    def body(i, blk_max):
        j = per_trip * i
        for u in range(per_trip):
            nxt = produce(jnp.minimum(j + u + 1, n_blocks - 1), (u + 1) % 2)
            consume(j + u, u % 2, blk_max)
            blk_max = nxt
        return blk_max

    lax.fori_loop(0, n_blocks // per_trip, body, produce(0, 0))


def _attn_a_kernel(q_ref, k_ref, v_ref, lq1_ref, lk1_ref, lq2_ref, lk2_ref, sub_ref, o_ref,
                   s_scr, m_scr, l_scr, acc_scr, *, lam_init, seq):
    qt = q_ref[0].astype(F32)
    row = lax.broadcasted_iota(jnp.int32, qt.shape, 0)
    zero = jnp.zeros_like(qt)
    w = jnp.concatenate([jnp.where(row < HEAD_DIM, qt, zero), jnp.where(row >= HEAD_DIM, qt, zero)],
                        axis=1).astype(BF16)
    blk = lambda j: pl.ds(pl.multiple_of(j * TK, TK), TK)
    _flash_loop(lambda j: _dot(k_ref[0, 0, blk(j), :], w),
                lambda j: v_ref[0, :, blk(j)], seq // TK, s_scr, m_scr, l_scr, acc_scr)

    o = acc_scr[...] / l_scr[...]
    lam = (jnp.exp(jnp.sum(lq1_ref[...] * lk1_ref[...], axis=-1, keepdims=True))
           - jnp.exp(jnp.sum(lq2_ref[...] * lk2_ref[...], axis=-1, keepdims=True)) + lam_init)
    d = o[:, :TQ_A] - lam * o[:, TQ_A:]
    y = d * lax.rsqrt(jnp.mean(d * d, axis=0, keepdims=True) + EPS) * sub_ref[...]
    o_ref[0] = (y * (1.0 - lam_init)).T.astype(BF16)


def _attn_a(qta, ka, vta, lq1, lk1, lq2, lk2, subln, lam_init):
    batch, _, seq = qta.shape
    vec = _resident((1, HEAD_DIM))
    return pl.pallas_call(
        functools.partial(_attn_a_kernel, lam_init=lam_init, seq=seq),
        grid=(batch, A_HEADS, seq // TQ_A),
        in_specs=[pl.BlockSpec((1, 128, TQ_A), lambda b, h, q: (b, h, q)),
                  pl.BlockSpec((1, 1, seq, 128), lambda b, h, q: (b, h, 0, 0)),
                  pl.BlockSpec((1, 128, seq), lambda b, h, q: (b, h, 0)),
                  vec, vec, vec, vec,
                  _resident((128, 1))],
        out_specs=pl.BlockSpec((1, TQ_A, 128), lambda b, h, q: (b, q, h)),
        out_shape=jax.ShapeDtypeStruct((batch, seq, A_V), BF16),
        scratch_shapes=[pltpu.VMEM((2, TK, 2 * TQ_A), F32), pltpu.VMEM((1, 2 * TQ_A), F32),
                        pltpu.VMEM((1, 2 * TQ_A), F32), pltpu.VMEM((128, 2 * TQ_A), F32)],
        compiler_params=_params(3),
        name="attn_diff",
    )(qta, ka, vta, lq1, lk1, lq2, lk2, subln)


def _attn_b_kernel(q_ref, k_ref, v_ref, o_ref, s_scr, m_scr, acc_scr, *, seq):
    g = pl.program_id(1)
    rep = B_Q_HEADS // B_KV_HEADS
    qt = q_ref[0].astype(F32)
    wq = jnp.concatenate([qt[r * 64:(r + 1) * 64] for r in range(rep)], axis=1)
    zero = jnp.zeros_like(wq)
    w = jnp.concatenate([jnp.where(g == gg, wq, zero) for gg in range(B_KV_HEADS)], axis=0).astype(BF16)
    blk = lambda j: pl.ds(pl.multiple_of(j * TK, TK), TK)
    _flash_loop(lambda j: _dot(k_ref[0, blk(j), :], w),
                lambda j: v_ref[0, 0, :, blk(j)], seq // TK, s_scr, m_scr, None, acc_scr)

    o = acc_scr[0:HEAD_DIM, :] / acc_scr[HEAD_DIM:HEAD_DIM + 1, :]
    ot = jnp.concatenate([o[:, r * TQ_B:(r + 1) * TQ_B] for r in range(rep)], axis=0)
    o_ref[0] = ot.T.astype(BF16)


def _attn_b(qtb, kb, vtb):
    batch, _, seq = qtb.shape
    rep = B_Q_HEADS // B_KV_HEADS
    n = rep * TQ_B
    return pl.pallas_call(
        functools.partial(_attn_b_kernel, seq=seq),
        grid=(batch, B_KV_HEADS, seq // TQ_B),
        in_specs=[pl.BlockSpec((1, rep * HEAD_DIM, TQ_B), lambda b, g, q: (b, g, q)),
                  pl.BlockSpec((1, seq, B_KV), lambda b, g, q: (b, 0, 0)),
                  pl.BlockSpec((1, 1, VB_ROWS, seq), lambda b, g, q: (b, g, 0, 0))],
        out_specs=pl.BlockSpec((1, TQ_B, rep * HEAD_DIM), lambda b, g, q: (b, q, g)),
        out_shape=jax.ShapeDtypeStruct((batch, seq, B_Q), BF16),
        scratch_shapes=[pltpu.VMEM((2, TK, n), F32), pltpu.VMEM((1, n), F32), pltpu.VMEM((VB_ROWS, n), F32)],
        compiler_params=_params(3),
        name="attn_gqa",
    )(qtb, kb, vtb)


def _proj_odd_kernel(x_ref, g_ref, wqt_ref, wk_ref, wvt_ref, qt_ref, k_ref, vt_ref):
    h = _rms(x_ref[...], g_ref[...]).astype(BF16)
    qt_ref[0] = (lax.dot_general(wqt_ref[...], h, _NT, preferred_element_type=F32) * Q_SCALE).astype(BF16)
    k_ref[0] = _dot(h, wk_ref[...]).astype(BF16)
    vt_ref[0] = lax.dot_general(wvt_ref[...], h, _NT, preferred_element_type=F32).astype(BF16)


def _proj_odd(xf, g, wqt, wk, wvt, batch, seq):
    t = xf.shape[0]
    ns = seq // TM
    tok_t = pl.BlockSpec((1, C_MIX, TM), lambda i: (i // ns, 0, i % ns))
    sq = _resident((C_MIX, D_MODEL))
    return pl.pallas_call(
        _proj_odd_kernel,
        grid=(t // TM,),
        in_specs=[pl.BlockSpec((TM, D_MODEL), lambda i: (i, 0)), _resident((1, D_MODEL)),
                  sq, _resident((D_MODEL, C_MIX)), sq],
        out_specs=(tok_t, pl.BlockSpec((1, TM, C_MIX), lambda i: (i // ns, i % ns, 0)), tok_t),
        out_shape=(jax.ShapeDtypeStruct((batch, C_MIX, seq), BF16),
                   jax.ShapeDtypeStruct((batch, seq, C_MIX), BF16),
                   jax.ShapeDtypeStruct((batch, C_MIX, seq), BF16)),
        compiler_params=_params(1),
        name="proj_odd",
    )(xf, g, wqt, wk, wvt)


def _na_bias_kernel(tab_ref, o_ref):
    h = pl.program_id(0)
    shp = (GRID_W, 2 * GRID_W)
    kc = lax.broadcasted_iota(jnp.int32, shp, 0)
    lane = lax.broadcasted_iota(jnp.int32, shp, 1)
    qc = lane & (GRID_W - 1)
    cs = jnp.clip(qc - NA_KW // 2, 0, GRID_W - NA_KW)
    col_valid = (kc >= cs) & (kc < cs + NA_KW)
    dci = jnp.clip(kc - qc + NA_KW - 1, 0, 2 * NA_KW - 2)
    neg = jnp.full(shp, -jnp.inf, F32)
    n_dc = 2 * NA_KW - 1
    n_dr = 2 * NA_KH - 1
    planes = []
    for dr in range(n_dr):
        acc = jnp.zeros(shp, F32)
        for dd in range(n_dc):
            acc = jnp.where(dci == dd, tab_ref[h, dr * n_dc + dd], acc)
        planes.append(jnp.where(col_valid, acc * LOG2E, neg))
    for i in range(NA_PT):
        dl, dr_ = i - NA_PT_OFF, i - NA_PT_OFF - 1
        left = planes[dl] if 0 <= dl < n_dr else neg
        right = planes[dr_] if 0 <= dr_ < n_dr else neg
        o_ref[0, i] = jnp.where(lane < GRID_W, left, right)


def _na_bias(rel_bias):
    tab = rel_bias.reshape(C_HEADS, -1)
    return pl.pallas_call(
        _na_bias_kernel,
        grid=(C_HEADS,),
        in_specs=[pl.BlockSpec(memory_space=pltpu.SMEM)],
        out_specs=pl.BlockSpec((1, NA_PT, GRID_W, 2 * GRID_W), lambda h: (h, 0, 0, 0)),
        out_shape=jax.ShapeDtypeStruct((C_HEADS, NA_PT, GRID_W, 2 * GRID_W), F32),
        compiler_params=_params(1),
        name="na_bias_table",
    )(tab)


def _na_window_start(r0, rows):
    return jnp.clip(r0 - NA_KH // 2, 0, rows - NA_WIN)


def _na_mask_kernel(o_ref, *, rows):
    c = pl.program_id(0)
    r0 = jnp.where(c == 0, 0, jnp.where(c == 1, NA_ROWS, rows - NA_ROWS))
    ws = _na_window_start(r0, rows)
    shp = (NA_WIN * GRID_W, NA_ROWS * GRID_W)
    kr = ws + lax.broadcasted_iota(jnp.int32, shp, 0) // GRID_W
    r = r0 + lax.broadcasted_iota(jnp.int32, shp, 1) // GRID_W
    rs = jnp.clip(r - NA_KH // 2, 0, rows - NA_KH)
    o_ref[0] = jnp.where((kr >= rs) & (kr < rs + NA_KH), 0.0, -jnp.inf).astype(F32)


def _na_mask(rows):
    shp = (NA_WIN * GRID_W, NA_ROWS * GRID_W)
    return pl.pallas_call(
        functools.partial(_na_mask_kernel, rows=rows),
        grid=(3,),
        out_specs=pl.BlockSpec((1,) + shp, lambda c: (c, 0, 0)),
        out_shape=jax.ShapeDtypeStruct((3,) + shp, F32),
        compiler_params=_params(1),
        name="na_row_mask",
    )()


def _na_kernel(q_ref, k_ref, v_ref, pt_ref, mask_ref, o_ref, s_scr, *, rows):
    nk = NA_WIN * GRID_W
    nq = NA_ROWS * GRID_W
    n_groups = rows // NA_ROWS
    hrow = lax.broadcasted_iota(jnp.int32, (2 * HEAD_DIM, nq), 0) // HEAD_DIM

    def window(g):
        ws = _na_window_start(g * NA_ROWS, rows)
        return ws, pl.ds(pl.multiple_of(ws * GRID_W, 2 * GRID_W), nk)

    def produce(g, slot):
        ws, win = window(g)
        qt = q_ref[0, :, pl.ds(pl.multiple_of(g * nq, nq), nq)].astype(F32)
        zero = jnp.zeros_like(qt)
        w = jnp.concatenate([jnp.where(hrow == hh, qt, zero) for hh in range(2)], axis=1).astype(BF16)
        s_t = _dot(k_ref[0, win, :], w)
        case = jnp.where(g == 0, 0, jnp.where(g == n_groups - 1, 2, 1))
        for j in range(NA_WIN):
            rsl = slice(j * GRID_W, (j + 1) * GRID_W)
            for hh in range(2):
                for a0 in range(0, NA_ROWS, 2):
                    idx = ws - g * NA_ROWS + (j - a0 + NA_KH - 1 + NA_PT_OFF)
                    msl = slice(a0 * GRID_W, (a0 + 2) * GRID_W)
                    csl = slice(hh * nq + a0 * GRID_W, hh * nq + (a0 + 2) * GRID_W)
                    s_scr[slot, rsl, csl] = s_t[rsl, csl] + pt_ref[hh, idx] + mask_ref[case, rsl, msl]
        return jnp.max(s_scr[slot], axis=0, keepdims=True)

    def consume(g, slot, blk_max):
        _, win = window(g)
        p = jnp.exp2(s_scr[slot] - blk_max)
        l = jnp.sum(p, axis=0, keepdims=True)
        ot = _dot(v_ref[0, :, win], p.astype(BF16)) / l
        o = jnp.concatenate([ot[hh * HEAD_DIM:(hh + 1) * HEAD_DIM, hh * nq:(hh + 1) * nq] for hh in range(2)],
                            axis=0)
        o_ref[0, pl.ds(pl.multiple_of(g * nq, nq), nq), :] = o.T.astype(BF16)

    per_trip = min(NA_GROUPS_PER_TRIP, n_groups)
    assert per_trip % 2 == 0 and n_groups % per_trip == 0

    def body(i, blk_max):
        g = per_trip * i
        for u in range(per_trip):
            nxt = produce(jnp.minimum(g + u + 1, n_groups - 1), (u + 1) % 2)
            consume(g + u, u % 2, blk_max)
            blk_max = nxt
        return blk_max

    lax.fori_loop(0, n_groups // per_trip, body, produce(0, 0))


def _na(qt, k, vt, pt, mask):
    batch, seq, _ = k.shape
    rows = seq // GRID_W
    assert rows % 2 == 0 and rows >= NA_WIN + NA_ROWS
    nq = NA_ROWS * GRID_W
    pairs = C_HEADS // 2
    tr = pl.BlockSpec((1, 2 * HEAD_DIM, seq), lambda b, h: (b, h, 0))
    nat = pl.BlockSpec((1, seq, 2 * HEAD_DIM), lambda b, h: (b, 0, h))
    return pl.pallas_call(
        functools.partial(_na_kernel, rows=rows),
        grid=(batch, pairs),
        in_specs=[tr, nat, tr,
                  pl.BlockSpec((2, NA_PT, GRID_W, 2 * GRID_W), lambda b, h: (h, 0, 0, 0)),
                  _resident((3, NA_WIN * GRID_W, nq))],
        out_specs=nat,
        out_shape=jax.ShapeDtypeStruct((batch, seq, C_MIX), BF16),
        scratch_shapes=[pltpu.VMEM((2, NA_WIN * GRID_W, 2 * nq), F32)],
        compiler_params=_params(2),
        name="attn_nbr",
    )(qt, k, vt, pt, mask)


def _outproj_kernel(*refs):
    n = (len(refs) - 2) // 2
    x_ref, o_ref = refs[0], refs[-1]
    y = x_ref[...]
    for a_ref, w_ref in zip(refs[1:1 + n], refs[1 + n:1 + 2 * n]):
        y = y + _dot(a_ref[...], w_ref[...])
    o_ref[...] = y


def _outproj(xf, mixes, weights):
    t = xf.shape[0]
    tile = lambda c: pl.BlockSpec((TM, c), lambda i: (i, 0))
    return pl.pallas_call(
        _outproj_kernel,
        grid=(t // TM,),
        in_specs=[tile(D_MODEL)] + [tile(m.shape[1]) for m in mixes] + [_resident(w.shape) for w in weights],
        out_specs=tile(D_MODEL),
        out_shape=jax.ShapeDtypeStruct((t, D_MODEL), F32),
        compiler_params=_params(1),
        name="out_proj",
    )(xf, *mixes, *weights)


def _ffn_kernel(xm_ref, xp_ref, xn_ref, g_ref, wu_ref, cw_ref, cb_ref, wd_ref, pg_ref, wg_ref, p_ref, wp_ref,
                fg_ref, o_ref, h_scr, acc_scr, *, tiles_per_seq, final):
    i = pl.program_id(0)
    g = g_ref[...]
    xm = xm_ref[...]
    pos = i % tiles_per_seq
    hp = jnp.where(pos == 0, 0.0, _rms(xp_ref[...], g))
    hn = jnp.where(pos == tiles_per_seq - 1, 0.0, _rms(xn_ref[...], g))
    h_scr[...] = jnp.concatenate([hp, _rms(xm, g), hn], axis=0).astype(BF16)
    acc_scr[...] = jnp.zeros(acc_scr.shape, F32)

    def conv(u, col):
        cw = cw_ref[:, pl.ds(col, TF)]
        rows = u.shape[0]
        y = (pltpu.roll(u, 1, 0) * cw[0:1] + u * cw[1:2] + pltpu.roll(u, rows - 1, 0) * cw[2:3]
             + cb_ref[:, pl.ds(col, TF)])
        return y[HALO:HALO + TM]

    for c in range(D_FF // TF):
        off = c * TF
        h = h_scr[...]
        gate = conv(_dot(h, wu_ref[:, pl.ds(off, TF)]), off)
        val = conv(_dot(h, wu_ref[:, pl.ds(D_FF + off, TF)]), D_FF + off)
        act = gate * jax.nn.sigmoid(gate) * val
        acc_scr[...] += _dot(act.astype(BF16), wd_ref[pl.ds(off, TF), :])

    x2 = xm + acc_scr[...]
    gate = jax.nn.sigmoid(_dot(_rms(x2, pg_ref[...]).astype(BF16), wg_ref[...]))
    x3 = x2 + gate * _dot(p_ref[...].astype(BF16), wp_ref[...])
    if final:
        x3 = _rms(x3, fg_ref[...])
    o_ref[...] = x3


def _ffn(xf, g, wu, cw, cb, wd, pg, wg, p, wp, fg, seq, final):
    t = xf.shape[0]
    tps = seq // TM
    nb = TM // HALO
    last = t // HALO - 1
    return pl.pallas_call(
        functools.partial(_ffn_kernel, tiles_per_seq=tps, final=final),
        grid=(t // TM,),
        in_specs=[pl.BlockSpec((TM, D_MODEL), lambda i: (i, 0)),
                  pl.BlockSpec((HALO, D_MODEL), lambda i: (jnp.maximum(i * nb - 1, 0), 0)),
                  pl.BlockSpec((HALO, D_MODEL), lambda i: (jnp.minimum((i + 1) * nb, last), 0)),
                  _resident((1, D_MODEL)),
                  _resident((D_MODEL, 2 * D_FF)),
                  _resident((3, 2 * D_FF)), _resident((1, 2 * D_FF)),
                  _resident((D_FF, D_MODEL)),
                  _resident((1, D_MODEL)),
                  _resident((D_MODEL, D_MODEL)),
                  pl.BlockSpec((TM, PLE_DIM), lambda i: (i, 0)),
                  _resident((PLE_DIM, D_MODEL)),
                  _resident((1, D_MODEL))],
        out_specs=pl.BlockSpec((TM, D_MODEL), lambda i: (i, 0)),
        out_shape=jax.ShapeDtypeStruct((t, D_MODEL), F32),
        scratch_shapes=[pltpu.VMEM((TM + 2 * HALO, D_MODEL), BF16), pltpu.VMEM((TM, D_MODEL), F32)],
        compiler_params=_params(1),
        name="convffn_ple",
    )(xf, xf, xf, g, wu, cw, cb, wd, pg, wg, p, wp, fg)


def _rope_tables(seq):
    pos = jnp.arange(seq, dtype=jnp.int32)

    def tab(p, half, theta):
        inv = theta ** (-jnp.arange(half, dtype=F32) / half)
        ang = inv[:, None] * p.astype(F32)[None, :]
        return jnp.cos(ang), jnp.sin(ang)

    ca, sa = tab(pos, HEAD_DIM // 8, ROPE_THETA)
    cr, sr = tab(pos // GRID_W, HEAD_DIM // 4, AXIAL_THETA)
    cc, sc = tab(pos % GRID_W, HEAD_DIM // 4, AXIAL_THETA)
    return ca, sa, cr, sr, cc, sc


def kernel(x, p, attn_norm, w_in_ab, lambda_q1, lambda_k1, lambda_q2, lambda_k2, a_subln, b_q_norm, b_k_norm,
           w_out_ab, w_in_c, c_rel_bias, w_out_c, ffn_norm, w_ffn_up, ffn_conv_w, ffn_conv_b, w_ffn_down,
           ple_norm, w_ple_gate, w_ple_proj, final_norm):
    batch, seq, _ = x.shape
    depth = attn_norm.shape[0]
    t = batch * seq
    xf = x.reshape(t, D_MODEL)
    tabs = _rope_tables(seq)
    row = lambda v: v.reshape(1, -1)
    colv = lambda v: v.reshape(-1, 1)
    for i in range(depth):
        j = i // 2
        if i % 2 == 0:
            lam_init = 0.8 - 0.6 * math.exp(-0.3 * i)
            wt = w_in_ab[j].T.astype(BF16)
            qta, ka, vta, qtb, kb, vtb = _proj_even(xf, row(attn_norm[i]), wt, tabs, colv(b_q_norm[j]),
                                                    colv(b_k_norm[j]), batch, seq)
            mix_a = _attn_a(qta, ka, vta, row(lambda_q1[j]), row(lambda_k1[j]), row(lambda_q2[j]),
                            row(lambda_k2[j]), colv(a_subln[j]), lam_init)
            mix_b = _attn_b(qtb, kb, vtb)
            w_out = w_out_ab[j].astype(BF16)
            x1 = _outproj(xf, [mix_a.reshape(t, A_V), mix_b.reshape(t, B_Q)], [w_out[:A_V], w_out[A_V:]])
        else:
            w_c = w_in_c[j].astype(BF16)
            qt, k, vt = _proj_odd(xf, row(attn_norm[i]), w_c[:, :C_MIX].T, w_c[:, C_MIX:2 * C_MIX],
                                  w_c[:, 2 * C_MIX:].T, batch, seq)
            o = _na(qt, k, vt, _na_bias(c_rel_bias[j]), _na_mask(seq // GRID_W))
            x1 = _outproj(xf, [o.reshape(t, C_MIX)], [w_out_c[j].astype(BF16)])
        xf = _ffn(x1, row(ffn_norm[i]), w_ffn_up[i].astype(BF16), ffn_conv_w[i], row(ffn_conv_b[i]),
                  w_ffn_down[i].astype(BF16), row(ple_norm[i]), w_ple_gate[i].astype(BF16),
                  p[i].reshape(t, PLE_DIM), w_ple_proj[i].astype(BF16), row(final_norm), seq,
                  final=(i == depth - 1))
    return xf.reshape(batch, seq, D_MODEL)
```

```python
import functools
import math

import jax
import jax.numpy as jnp
from jax import lax
from jax.experimental import pallas as pl
from jax.experimental.pallas import tpu as pltpu

F32 = jnp.float32
BF16 = jnp.bfloat16

D_MODEL = 1024
HEAD_DIM = 64
A_HEADS = 4
B_Q_HEADS = 8
B_KV_HEADS = 2
C_HEADS = 16
ROPE_THETA = 500000.0
AXIAL_THETA = 10000.0
GRID_W = 64
NA_KH = 8
NA_KW = 16
D_FF = 2816
PLE_DIM = 256
EPS = 1e-6
SCALE = HEAD_DIM ** -0.5
LOG2E = math.log2(math.e)
Q_SCALE = SCALE * LOG2E

A_QK = A_HEADS * 2 * HEAD_DIM
A_V = A_HEADS * 2 * HEAD_DIM
B_Q = B_Q_HEADS * HEAD_DIM
B_KV = B_KV_HEADS * HEAD_DIM
EVEN_IN = 2 * A_QK + A_V + B_Q + 2 * B_KV
C_MIX = C_HEADS * HEAD_DIM

TM = 512
TQ_A = 512
TQ_B = 256
TK_A = 512
TK_B = 256
VB_ROWS = HEAD_DIM + 16
KEYS_PER_TRIP = 4096
SCORE_SLOTS = 2
TF = 256
FFN_SLOTS = 3
HALO = 8
NA_ROWS = 4
NA_WIN = NA_ROWS + NA_KH
NA_PT_OFF = NA_WIN - NA_KH - 1
NA_PT = NA_WIN + NA_KH - 1 + NA_PT_OFF
NA_GROUPS_PER_TRIP = 4
VMEM_LIMIT = 52 * 1024 * 1024

_NT = (((1,), (1,)), ((), ()))


def _dot(a, b):
    return jnp.dot(a, b, preferred_element_type=F32)


def _rms(xf, g):
    return xf * lax.rsqrt(jnp.mean(xf * xf, axis=-1, keepdims=True) + EPS) * g


def _params(n_axes, limit=VMEM_LIMIT):
    return pltpu.CompilerParams(dimension_semantics=("arbitrary",) * n_axes, vmem_limit_bytes=limit)


def _resident(shape):
    nd = len(shape)
    return pl.BlockSpec(shape, lambda *_: (0,) * nd, pipeline_mode=pl.Buffered(1))


def _proj_even_kernel(x_ref, g_ref, wt_ref, ca_ref, sa_ref, cr_ref, sr_ref, cc_ref, sc_ref, qn_ref, kn_ref,
                      qta_ref, ka_ref, vta_ref, qtb_ref, kb_ref, vtb_ref):
    h = _rms(x_ref[...], g_ref[...]).astype(BF16)

    def proj_t(lo, hi):
        return lax.dot_general(wt_ref[lo:hi, :], h, _NT, preferred_element_type=F32)

    ca, sa = ca_ref[...], sa_ref[...]
    cr, sr = cr_ref[...], sr_ref[...]
    cc, sc = cc_ref[...], sc_ref[...]

    def rope_a(blk):
        x1, x2 = blk[0:8], blk[8:16]
        return jnp.concatenate([x1 * ca - x2 * sa, x2 * ca + x1 * sa, blk[16:]], axis=0)

    def norm_rope_b(blk, gain):
        n = blk * lax.rsqrt(jnp.mean(blk * blk, axis=0, keepdims=True) + EPS) * gain
        r1, r2, c1, c2 = n[0:16], n[16:32], n[32:48], n[48:64]
        return jnp.concatenate([r1 * cr - r2 * sr, r2 * cr + r1 * sr,
                                c1 * cc - c2 * sc, c2 * cc + c1 * sc], axis=0)

    o = 0
    aq = proj_t(o, o + A_QK)
    qta_ref[0] = (jnp.concatenate([rope_a(aq[m * 64:(m + 1) * 64]) for m in range(2 * A_HEADS)], axis=0)
                  * Q_SCALE).astype(BF16)
    o += A_QK
    ak = proj_t(o, o + A_QK)
    for hh in range(A_HEADS):
        kt = jnp.concatenate([rope_a(ak[(2 * hh + m) * 64:(2 * hh + m + 1) * 64]) for m in range(2)], axis=0)
        ka_ref[0, hh] = kt.T.astype(BF16)
    o += A_QK
    vta_ref[0] = proj_t(o, o + A_V).astype(BF16)
    o += A_V
    bq = proj_t(o, o + B_Q)
    qn = qn_ref[...]
    qtb_ref[0] = (jnp.concatenate([norm_rope_b(bq[m * 64:(m + 1) * 64], qn) for m in range(B_Q_HEADS)], axis=0)
                  * Q_SCALE).astype(BF16)
    o += B_Q
    bk = proj_t(o, o + B_KV)
    kn = kn_ref[...]
    kbt = jnp.concatenate([norm_rope_b(bk[m * 64:(m + 1) * 64], kn) for m in range(B_KV_HEADS)], axis=0)
    kb_ref[0] = kbt.T.astype(BF16)
    o += B_KV
    bv = proj_t(o, o + B_KV)
    pad = jnp.concatenate([jnp.ones((8, TM), F32), jnp.zeros((VB_ROWS - HEAD_DIM - 8, TM), F32)], axis=0)
    for gg in range(B_KV_HEADS):
        vtb_ref[0, gg] = jnp.concatenate([bv[gg * 64:(gg + 1) * 64], pad], axis=0).astype(BF16)


def _proj_even(xf, g, wt, tabs, qn, kn, batch, seq):
    t = xf.shape[0]
    ns = seq // TM
    tok = lambda i: (i // ns, 0, i % ns)
    tab = lambda rows: pl.BlockSpec((rows, TM), lambda i: (0, i % ns))
    out_shape = (
        jax.ShapeDtypeStruct((batch, A_QK, seq), BF16),
        jax.ShapeDtypeStruct((batch, A_HEADS, seq, 128), BF16),
        jax.ShapeDtypeStruct((batch, A_V, seq), BF16),
        jax.ShapeDtypeStruct((batch, B_Q, seq), BF16),
        jax.ShapeDtypeStruct((batch, seq, B_KV), BF16),
        jax.ShapeDtypeStruct((batch, B_KV_HEADS, VB_ROWS, seq), BF16),
    )
    return pl.pallas_call(
        _proj_even_kernel,
        grid=(t // TM,),
        in_specs=[pl.BlockSpec((TM, D_MODEL), lambda i: (i, 0)),
                  _resident((1, D_MODEL)),
                  _resident((EVEN_IN, D_MODEL)),
                  tab(8), tab(8), tab(16), tab(16), tab(16), tab(16),
                  _resident((HEAD_DIM, 1)), _resident((HEAD_DIM, 1))],
        out_specs=(pl.BlockSpec((1, A_QK, TM), tok),
                   pl.BlockSpec((1, A_HEADS, TM, 128), lambda i: (i // ns, 0, i % ns, 0)),
                   pl.BlockSpec((1, A_V, TM), tok),
                   pl.BlockSpec((1, B_Q, TM), tok),
                   pl.BlockSpec((1, TM, B_KV), lambda i: (i // ns, i % ns, 0)),
                   pl.BlockSpec((1, B_KV_HEADS, VB_ROWS, TM), lambda i: (i // ns, 0, 0, i % ns))),
        out_shape=out_shape,
        compiler_params=_params(1),
        name="proj_even",
    )(xf, g, wt, *tabs, qn, kn)


def _flash_loop(scores_fn, values_fn, n_blocks, s_scr, m_scr, l_scr, acc_scr):
    m_scr[...] = jnp.full(m_scr.shape, -jnp.inf, F32)
    acc_scr[...] = jnp.zeros(acc_scr.shape, F32)
    if l_scr is not None:
        l_scr[...] = jnp.zeros(l_scr.shape, F32)

    def produce(j, slot):
        s_t = scores_fn(j)
        s_scr[slot] = s_t
        return jnp.max(s_t, axis=0, keepdims=True)

    def consume(j, slot, blk_max):
        m_old = m_scr[...]
        m_new = jnp.maximum(m_old, blk_max)
        alpha = jnp.exp2(m_old - m_new)
        p = jnp.exp2(s_scr[slot] - m_new)
        if l_scr is not None:
            l_scr[...] = alpha * l_scr[...] + jnp.sum(p, axis=0, keepdims=True)
        acc_scr[...] = alpha * acc_scr[...] + _dot(values_fn(j), p.astype(BF16))
        m_scr[...] = m_new

    n_slots = s_scr.shape[0]
    per_trip = min(KEYS_PER_TRIP // s_scr.shape[1], n_blocks)
    assert per_trip % n_slots == 0 and n_blocks % per_trip == 0

    def body(i, blk_max):
        j = per_trip * i
        for u in range(per_trip):
            nxt = produce(jnp.minimum(j + u + 1, n_blocks - 1), (u + 1) % n_slots)
            consume(j + u, u % n_slots, blk_max)
            blk_max = nxt
        return blk_max

    lax.fori_loop(0, n_blocks // per_trip, body, produce(0, 0))


def _attn_a_kernel(q_ref, k_ref, v_ref, lq1_ref, lk1_ref, lq2_ref, lk2_ref, sub_ref, o_ref,
                   s_scr, m_scr, l_scr, acc_scr, *, lam_init, seq):
    qt = q_ref[0].astype(F32)
    row = lax.broadcasted_iota(jnp.int32, qt.shape, 0)
    zero = jnp.zeros_like(qt)
    w = jnp.concatenate([jnp.where(row < HEAD_DIM, qt, zero), jnp.where(row >= HEAD_DIM, qt, zero)],
                        axis=1).astype(BF16)
    blk = lambda j: pl.ds(pl.multiple_of(j * TK_A, TK_A), TK_A)
    _flash_loop(lambda j: _dot(k_ref[0, 0, blk(j), :], w),
                lambda j: v_ref[0, :, blk(j)], seq // TK_A, s_scr, m_scr, l_scr, acc_scr)

    o = acc_scr[...] / l_scr[...]
    lam = (jnp.exp(jnp.sum(lq1_ref[...] * lk1_ref[...], axis=-1, keepdims=True))
           - jnp.exp(jnp.sum(lq2_ref[...] * lk2_ref[...], axis=-1, keepdims=True)) + lam_init)
    d = o[:, :TQ_A] - lam * o[:, TQ_A:]
    y = d * lax.rsqrt(jnp.mean(d * d, axis=0, keepdims=True) + EPS) * sub_ref[...]
    o_ref[0] = (y * (1.0 - lam_init)).T.astype(BF16)


def _attn_a(qta, ka, vta, lq1, lk1, lq2, lk2, subln, lam_init):
    batch, _, seq = qta.shape
    vec = _resident((1, HEAD_DIM))
    return pl.pallas_call(
        functools.partial(_attn_a_kernel, lam_init=lam_init, seq=seq),
        grid=(batch, A_HEADS, seq // TQ_A),
        in_specs=[pl.BlockSpec((1, 128, TQ_A), lambda b, h, q: (b, h, q)),
                  pl.BlockSpec((1, 1, seq, 128), lambda b, h, q: (b, h, 0, 0)),
                  pl.BlockSpec((1, 128, seq), lambda b, h, q: (b, h, 0)),
                  vec, vec, vec, vec,
                  _resident((128, 1))],
        out_specs=pl.BlockSpec((1, TQ_A, 128), lambda b, h, q: (b, q, h)),
        out_shape=jax.ShapeDtypeStruct((batch, seq, A_V), BF16),
        scratch_shapes=[pltpu.VMEM((SCORE_SLOTS, TK_A, 2 * TQ_A), F32), pltpu.VMEM((1, 2 * TQ_A), F32),
                        pltpu.VMEM((1, 2 * TQ_A), F32), pltpu.VMEM((128, 2 * TQ_A), F32)],
        compiler_params=_params(3),
        name="attn_diff",
    )(qta, ka, vta, lq1, lk1, lq2, lk2, subln)


def _attn_b_kernel(q_ref, k_ref, v_ref, o_ref, s_scr, m_scr, acc_scr, *, seq):
    g = pl.program_id(1)
    rep = B_Q_HEADS // B_KV_HEADS
    qt = q_ref[0].astype(F32)
    wq = jnp.concatenate([qt[r * 64:(r + 1) * 64] for r in range(rep)], axis=1)
    zero = jnp.zeros_like(wq)
    w = jnp.concatenate([jnp.where(g == gg, wq, zero) for gg in range(B_KV_HEADS)], axis=0).astype(BF16)
    blk = lambda j: pl.ds(pl.multiple_of(j * TK_B, TK_B), TK_B)
    _flash_loop(lambda j: _dot(k_ref[0, blk(j), :], w),
                lambda j: v_ref[0, 0, :, blk(j)], seq // TK_B, s_scr, m_scr, None, acc_scr)

    o = acc_scr[0:HEAD_DIM, :] / acc_scr[HEAD_DIM:HEAD_DIM + 1, :]
    ot = jnp.concatenate([o[:, r * TQ_B:(r + 1) * TQ_B] for r in range(rep)], axis=0)
    o_ref[0] = ot.T.astype(BF16)


def _attn_b(qtb, kb, vtb):
    batch, _, seq = qtb.shape
    rep = B_Q_HEADS // B_KV_HEADS
    n = rep * TQ_B
    return pl.pallas_call(
        functools.partial(_attn_b_kernel, seq=seq),
        grid=(batch, B_KV_HEADS, seq // TQ_B),
        in_specs=[pl.BlockSpec((1, rep * HEAD_DIM, TQ_B), lambda b, g, q: (b, g, q)),
                  pl.BlockSpec((1, seq, B_KV), lambda b, g, q: (b, 0, 0)),
                  pl.BlockSpec((1, 1, VB_ROWS, seq), lambda b, g, q: (b, g, 0, 0))],
        out_specs=pl.BlockSpec((1, TQ_B, rep * HEAD_DIM), lambda b, g, q: (b, q, g)),
        out_shape=jax.ShapeDtypeStruct((batch, seq, B_Q), BF16),
        scratch_shapes=[pltpu.VMEM((SCORE_SLOTS, TK_B, n), F32), pltpu.VMEM((1, n), F32),
                        pltpu.VMEM((VB_ROWS, n), F32)],
        compiler_params=_params(3),
        name="attn_gqa",
    )(qtb, kb, vtb)


def _proj_odd_kernel(x_ref, g_ref, wqt_ref, wk_ref, wvt_ref, qt_ref, k_ref, vt_ref):
    h = _rms(x_ref[...], g_ref[...]).astype(BF16)
    qt_ref[0] = (lax.dot_general(wqt_ref[...], h, _NT, preferred_element_type=F32) * Q_SCALE).astype(BF16)
    k_ref[0] = _dot(h, wk_ref[...]).astype(BF16)
    vt_ref[0] = lax.dot_general(wvt_ref[...], h, _NT, preferred_element_type=F32).astype(BF16)


def _proj_odd(xf, g, wqt, wk, wvt, batch, seq):
    t = xf.shape[0]
    ns = seq // TM
    tok_t = pl.BlockSpec((1, C_MIX, TM), lambda i: (i // ns, 0, i % ns))
    sq = _resident((C_MIX, D_MODEL))
    return pl.pallas_call(
        _proj_odd_kernel,
        grid=(t // TM,),
        in_specs=[pl.BlockSpec((TM, D_MODEL), lambda i: (i, 0)), _resident((1, D_MODEL)),
                  sq, _resident((D_MODEL, C_MIX)), sq],
        out_specs=(tok_t, pl.BlockSpec((1, TM, C_MIX), lambda i: (i // ns, i % ns, 0)), tok_t),
        out_shape=(jax.ShapeDtypeStruct((batch, C_MIX, seq), BF16),
                   jax.ShapeDtypeStruct((batch, seq, C_MIX), BF16),
                   jax.ShapeDtypeStruct((batch, C_MIX, seq), BF16)),
        compiler_params=_params(1),
        name="proj_odd",
    )(xf, g, wqt, wk, wvt)


def _na_bias_kernel(tab_ref, o_ref):
    h = pl.program_id(0)
    shp = (GRID_W, 2 * GRID_W)
    kc = lax.broadcasted_iota(jnp.int32, shp, 0)
    lane = lax.broadcasted_iota(jnp.int32, shp, 1)
    qc = lane & (GRID_W - 1)
    cs = jnp.clip(qc - NA_KW // 2, 0, GRID_W - NA_KW)
    col_valid = (kc >= cs) & (kc < cs + NA_KW)
    dci = jnp.clip(kc - qc + NA_KW - 1, 0, 2 * NA_KW - 2)
    neg = jnp.full(shp, -jnp.inf, F32)
    n_dc = 2 * NA_KW - 1
    n_dr = 2 * NA_KH - 1
    planes = []
    for dr in range(n_dr):
        acc = jnp.zeros(shp, F32)
        for dd in range(n_dc):
            acc = jnp.where(dci == dd, tab_ref[h, dr * n_dc + dd], acc)
        planes.append(jnp.where(col_valid, acc * LOG2E, neg))
    for i in range(NA_PT):
        dl, dr_ = i - NA_PT_OFF, i - NA_PT_OFF - 1
        left = planes[dl] if 0 <= dl < n_dr else neg
        right = planes[dr_] if 0 <= dr_ < n_dr else neg
        o_ref[0, i] = jnp.where(lane < GRID_W, left, right)


def _na_bias(rel_bias):
    tab = rel_bias.reshape(C_HEADS, -1)
    return pl.pallas_call(
        _na_bias_kernel,
        grid=(C_HEADS,),
        in_specs=[pl.BlockSpec(memory_space=pltpu.SMEM)],
        out_specs=pl.BlockSpec((1, NA_PT, GRID_W, 2 * GRID_W), lambda h: (h, 0, 0, 0)),
        out_shape=jax.ShapeDtypeStruct((C_HEADS, NA_PT, GRID_W, 2 * GRID_W), F32),
        compiler_params=_params(1),
        name="na_bias_table",
    )(tab)


def _na_window_start(r0, rows):
    return jnp.clip(r0 - NA_KH // 2, 0, rows - NA_WIN)


def _na_mask_kernel(o_ref, *, rows):
    c = pl.program_id(0)
    r0 = jnp.where(c == 0, 0, jnp.where(c == 1, NA_ROWS, rows - NA_ROWS))
    ws = _na_window_start(r0, rows)
    shp = (NA_WIN * GRID_W, NA_ROWS * GRID_W)
    kr = ws + lax.broadcasted_iota(jnp.int32, shp, 0) // GRID_W
    r = r0 + lax.broadcasted_iota(jnp.int32, shp, 1) // GRID_W
    rs = jnp.clip(r - NA_KH // 2, 0, rows - NA_KH)
    o_ref[0] = jnp.where((kr >= rs) & (kr < rs + NA_KH), 0.0, -jnp.inf).astype(F32)


def _na_mask(rows):
    shp = (NA_WIN * GRID_W, NA_ROWS * GRID_W)
    return pl.pallas_call(
        functools.partial(_na_mask_kernel, rows=rows),
        grid=(3,),
        out_specs=pl.BlockSpec((1,) + shp, lambda c: (c, 0, 0)),
        out_shape=jax.ShapeDtypeStruct((3,) + shp, F32),
        compiler_params=_params(1),
        name="na_row_mask",
    )()


def _na_kernel(q_ref, k_ref, v_ref, pt_ref, mask_ref, o_ref, s_scr, *, rows):
    nk = NA_WIN * GRID_W
    nq = NA_ROWS * GRID_W
    n_groups = rows // NA_ROWS
    hrow = lax.broadcasted_iota(jnp.int32, (2 * HEAD_DIM, nq), 0) // HEAD_DIM

    def window(g):
        ws = _na_window_start(g * NA_ROWS, rows)
        return ws, pl.ds(pl.multiple_of(ws * GRID_W, 2 * GRID_W), nk)

    def produce(g, slot):
        ws, win = window(g)
        qt = q_ref[0, :, pl.ds(pl.multiple_of(g * nq, nq), nq)].astype(F32)
        zero = jnp.zeros_like(qt)
        w = jnp.concatenate([jnp.where(hrow == hh, qt, zero) for hh in range(2)], axis=1).astype(BF16)
        s_t = _dot(k_ref[0, win, :], w)
        case = jnp.where(g == 0, 0, jnp.where(g == n_groups - 1, 2, 1))
        for j in range(NA_WIN):
            rsl = slice(j * GRID_W, (j + 1) * GRID_W)
            for hh in range(2):
                for a0 in range(0, NA_ROWS, 2):
                    idx = ws - g * NA_ROWS + (j - a0 + NA_KH - 1 + NA_PT_OFF)
                    msl = slice(a0 * GRID_W, (a0 + 2) * GRID_W)
                    csl = slice(hh * nq + a0 * GRID_W, hh * nq + (a0 + 2) * GRID_W)
                    s_scr[slot, rsl, csl] = s_t[rsl, csl] + pt_ref[hh, idx] + mask_ref[case, rsl, msl]
        return jnp.max(s_scr[slot], axis=0, keepdims=True)

    def consume(g, slot, blk_max):
        _, win = window(g)
        p = jnp.exp2(s_scr[slot] - blk_max)
        l = jnp.sum(p, axis=0, keepdims=True)
        ot = _dot(v_ref[0, :, win], p.astype(BF16)) / l
        o = jnp.concatenate([ot[hh * HEAD_DIM:(hh + 1) * HEAD_DIM, hh * nq:(hh + 1) * nq] for hh in range(2)],
                            axis=0)
        o_ref[0, pl.ds(pl.multiple_of(g * nq, nq), nq), :] = o.T.astype(BF16)

    per_trip = min(NA_GROUPS_PER_TRIP, n_groups)
    assert per_trip % 2 == 0 and n_groups % per_trip == 0

    def body(i, blk_max):
        g = per_trip * i
        for u in range(per_trip):
            nxt = produce(jnp.minimum(g + u + 1, n_groups - 1), (u + 1) % 2)
            consume(g + u, u % 2, blk_max)
            blk_max = nxt
        return blk_max

    lax.fori_loop(0, n_groups // per_trip, body, produce(0, 0))


def _na(qt, k, vt, pt, mask):
    batch, seq, _ = k.shape
    rows = seq // GRID_W
    assert rows % 2 == 0 and rows >= NA_WIN + NA_ROWS
    nq = NA_ROWS * GRID_W
    pairs = C_HEADS // 2
    tr = pl.BlockSpec((1, 2 * HEAD_DIM, seq), lambda b, h: (b, h, 0))
    nat = pl.BlockSpec((1, seq, 2 * HEAD_DIM), lambda b, h: (b, 0, h))
    return pl.pallas_call(
        functools.partial(_na_kernel, rows=rows),
        grid=(batch, pairs),
        in_specs=[tr, nat, tr,
                  pl.BlockSpec((2, NA_PT, GRID_W, 2 * GRID_W), lambda b, h: (h, 0, 0, 0)),
                  _resident((3, NA_WIN * GRID_W, nq))],
        out_specs=nat,
        out_shape=jax.ShapeDtypeStruct((batch, seq, C_MIX), BF16),
        scratch_shapes=[pltpu.VMEM((2, NA_WIN * GRID_W, 2 * nq), F32)],
        compiler_params=_params(2),
        name="attn_nbr",
    )(qt, k, vt, pt, mask)


def _outproj_kernel(*refs):
    n = (len(refs) - 2) // 2
    x_ref, o_ref = refs[0], refs[-1]
    y = x_ref[...]
    for a_ref, w_ref in zip(refs[1:1 + n], refs[1 + n:1 + 2 * n]):
        y = y + _dot(a_ref[...], w_ref[...])
    o_ref[...] = y


def _outproj(xf, mixes, weights):
    t = xf.shape[0]
    tile = lambda c: pl.BlockSpec((TM, c), lambda i: (i, 0))
    return pl.pallas_call(
        _outproj_kernel,
        grid=(t // TM,),
        in_specs=[tile(D_MODEL)] + [tile(m.shape[1]) for m in mixes] + [_resident(w.shape) for w in weights],
        out_specs=tile(D_MODEL),
        out_shape=jax.ShapeDtypeStruct((t, D_MODEL), F32),
        compiler_params=_params(1),
        name="out_proj",
    )(xf, *mixes, *weights)


def _ffn_kernel(xm_ref, xp_ref, xn_ref, g_ref, wu_ref, cw_ref, cb_ref, wd_ref, pg_ref, wg_ref, p_ref, wp_ref,
                fg_ref, o_ref, h_scr, acc_scr, u_scr, *, tiles_per_seq, final):
    i = pl.program_id(0)
    g = g_ref[...]
    xm = xm_ref[...]
    pos = i % tiles_per_seq
    hp = jnp.where(pos == 0, 0.0, _rms(xp_ref[...], g))
    hn = jnp.where(pos == tiles_per_seq - 1, 0.0, _rms(xn_ref[...], g))
    h_scr[...] = jnp.concatenate([hp, _rms(xm, g), hn], axis=0).astype(BF16)
    acc_scr[...] = jnp.zeros(acc_scr.shape, F32)

    def conv(u, col):
        cw = cw_ref[:, pl.ds(col, TF)]
        rows = u.shape[0]
        y = (pltpu.roll(u, 1, 0) * cw[0:1] + u * cw[1:2] + pltpu.roll(u, rows - 1, 0) * cw[2:3]
             + cb_ref[:, pl.ds(col, TF)])
        return y[HALO:HALO + TM]

    def up(c, slot):
        h = h_scr[...]
        u_scr[slot, 0] = _dot(h, wu_ref[:, pl.ds(c * TF, TF)])
        u_scr[slot, 1] = _dot(h, wu_ref[:, pl.ds(D_FF + c * TF, TF)])

    def act_down(c, slot):
        gate = conv(u_scr[slot, 0], c * TF)
        val = conv(u_scr[slot, 1], D_FF + c * TF)
        act = gate * jax.nn.sigmoid(gate) * val
        acc_scr[...] += _dot(act.astype(BF16), wd_ref[pl.ds(c * TF, TF), :])

    n_chunks = D_FF // TF
    n_slots = u_scr.shape[0]
    for c in range(min(n_slots - 1, n_chunks)):
        up(c, c)
    for c in range(n_chunks):
        ahead = c + n_slots - 1
        if ahead < n_chunks:
            up(ahead, ahead % n_slots)
        act_down(c, c % n_slots)

    x2 = xm + acc_scr[...]
    gate = jax.nn.sigmoid(_dot(_rms(x2, pg_ref[...]).astype(BF16), wg_ref[...]))
    x3 = x2 + gate * _dot(p_ref[...].astype(BF16), wp_ref[...])
    if final:
        x3 = _rms(x3, fg_ref[...])
    o_ref[...] = x3


def _ffn(xf, g, wu, cw, cb, wd, pg, wg, p, wp, fg, seq, final):
    t = xf.shape[0]
    tps = seq // TM
    nb = TM // HALO
    last = t // HALO - 1
    return pl.pallas_call(
        functools.partial(_ffn_kernel, tiles_per_seq=tps, final=final),
        grid=(t // TM,),
        in_specs=[pl.BlockSpec((TM, D_MODEL), lambda i: (i, 0)),
                  pl.BlockSpec((HALO, D_MODEL), lambda i: (jnp.maximum(i * nb - 1, 0), 0)),
                  pl.BlockSpec((HALO, D_MODEL), lambda i: (jnp.minimum((i + 1) * nb, last), 0)),
                  _resident((1, D_MODEL)),
                  _resident((D_MODEL, 2 * D_FF)),
                  _resident((3, 2 * D_FF)), _resident((1, 2 * D_FF)),
                  _resident((D_FF, D_MODEL)),
                  _resident((1, D_MODEL)),
                  _resident((D_MODEL, D_MODEL)),
                  pl.BlockSpec((TM, PLE_DIM), lambda i: (i, 0)),
                  _resident((PLE_DIM, D_MODEL)),
                  _resident((1, D_MODEL))],
        out_specs=pl.BlockSpec((TM, D_MODEL), lambda i: (i, 0)),
        out_shape=jax.ShapeDtypeStruct((t, D_MODEL), F32),
        scratch_shapes=[pltpu.VMEM((TM + 2 * HALO, D_MODEL), BF16), pltpu.VMEM((TM, D_MODEL), F32),
                        pltpu.VMEM((FFN_SLOTS, 2, TM + 2 * HALO, TF), F32)],
        compiler_params=_params(1),
        name="convffn_ple",
    )(xf, xf, xf, g, wu, cw, cb, wd, pg, wg, p, wp, fg)


def _rope_tables(seq):
    pos = jnp.arange(seq, dtype=jnp.int32)

    def tab(p, half, theta):
        inv = theta ** (-jnp.arange(half, dtype=F32) / half)
        ang = inv[:, None] * p.astype(F32)[None, :]
        return jnp.cos(ang), jnp.sin(ang)

    ca, sa = tab(pos, HEAD_DIM // 8, ROPE_THETA)
    cr, sr = tab(pos // GRID_W, HEAD_DIM // 4, AXIAL_THETA)
    cc, sc = tab(pos % GRID_W, HEAD_DIM // 4, AXIAL_THETA)
    return ca, sa, cr, sr, cc, sc


def kernel(x, p, attn_norm, w_in_ab, lambda_q1, lambda_k1, lambda_q2, lambda_k2, a_subln, b_q_norm, b_k_norm,
           w_out_ab, w_in_c, c_rel_bias, w_out_c, ffn_norm, w_ffn_up, ffn_conv_w, ffn_conv_b, w_ffn_down,
           ple_norm, w_ple_gate, w_ple_proj, final_norm):
    batch, seq, _ = x.shape
    depth = attn_norm.shape[0]
    t = batch * seq
    xf = x.reshape(t, D_MODEL)
    tabs = _rope_tables(seq)
    row = lambda v: v.reshape(1, -1)
    colv = lambda v: v.reshape(-1, 1)
    for i in range(depth):
        j = i // 2
        if i % 2 == 0:
            lam_init = 0.8 - 0.6 * math.exp(-0.3 * i)
            wt = w_in_ab[j].T.astype(BF16)
            qta, ka, vta, qtb, kb, vtb = _proj_even(xf, row(attn_norm[i]), wt, tabs, colv(b_q_norm[j]),
                                                    colv(b_k_norm[j]), batch, seq)
            mix_a = _attn_a(qta, ka, vta, row(lambda_q1[j]), row(lambda_k1[j]), row(lambda_q2[j]),
                            row(lambda_k2[j]), colv(a_subln[j]), lam_init)
            mix_b = _attn_b(qtb, kb, vtb)
            w_out = w_out_ab[j].astype(BF16)
            x1 = _outproj(xf, [mix_a.reshape(t, A_V), mix_b.reshape(t, B_Q)], [w_out[:A_V], w_out[A_V:]])
        else:
            w_c = w_in_c[j].astype(BF16)
            qt, k, vt = _proj_odd(xf, row(attn_norm[i]), w_c[:, :C_MIX].T, w_c[:, C_MIX:2 * C_MIX],
                                  w_c[:, 2 * C_MIX:].T, batch, seq)
            o = _na(qt, k, vt, _na_bias(c_rel_bias[j]), _na_mask(seq // GRID_W))
            x1 = _outproj(xf, [o.reshape(t, C_MIX)], [w_out_c[j].astype(BF16)])
        xf = _ffn(x1, row(ffn_norm[i]), w_ffn_up[i].astype(BF16), ffn_conv_w[i], row(ffn_conv_b[i]),
                  w_ffn_down[i].astype(BF16), row(ple_norm[i]), w_ple_gate[i].astype(BF16),
                  p[i].reshape(t, PLE_DIM), w_ple_proj[i].astype(BF16), row(final_norm), seq,
                  final=(i == depth - 1))
    return xf.reshape(batch, seq, D_MODEL)
```

```python
import functools
import math

import jax
import jax.numpy as jnp
from jax import lax
from jax.experimental import pallas as pl
from jax.experimental.pallas import tpu as pltpu

F32 = jnp.float32
BF16 = jnp.bfloat16

D_MODEL = 1024
HEAD_DIM = 64
A_HEADS = 4
B_Q_HEADS = 8
B_KV_HEADS = 2
C_HEADS = 16
ROPE_THETA = 500000.0
AXIAL_THETA = 10000.0
GRID_W = 64
NA_KH = 8
NA_KW = 16
D_FF = 2816
PLE_DIM = 256
EPS = 1e-6
SCALE = HEAD_DIM ** -0.5
LOG2E = math.log2(math.e)
Q_SCALE = SCALE * LOG2E

A_QK = A_HEADS * 2 * HEAD_DIM
A_V = A_HEADS * 2 * HEAD_DIM
B_Q = B_Q_HEADS * HEAD_DIM
B_KV = B_KV_HEADS * HEAD_DIM
EVEN_IN = 2 * A_QK + A_V + B_Q + 2 * B_KV
C_MIX = C_HEADS * HEAD_DIM

TM = 512
TQ_A = 512
TQ_B = 256
TK_A = 512
TK_B = 256
VB_ROWS = HEAD_DIM + 16
SCORE_SLOTS = 2
TF = 256
FFN_SLOTS = 3
HALO = 8
NA_ROWS = 4
NA_WIN = NA_ROWS + NA_KH
NA_PT_OFF = NA_WIN - NA_KH - 1
NA_PT = NA_WIN + NA_KH - 1 + NA_PT_OFF
NA_GROUPS_PER_TRIP = 4
VMEM_LIMIT = 52 * 1024 * 1024

_NT = (((1,), (1,)), ((), ()))


def _dot(a, b):
    return jnp.dot(a, b, preferred_element_type=F32)


def _rms(xf, g):
    return xf * lax.rsqrt(jnp.mean(xf * xf, axis=-1, keepdims=True) + EPS) * g


def _params(n_axes, limit=VMEM_LIMIT):
    return pltpu.CompilerParams(dimension_semantics=("arbitrary",) * n_axes, vmem_limit_bytes=limit)


def _resident(shape):
    nd = len(shape)
    return pl.BlockSpec(shape, lambda *_: (0,) * nd, pipeline_mode=pl.Buffered(1))


def _proj_even_kernel(x_ref, g_ref, wt_ref, ca_ref, sa_ref, cr_ref, sr_ref, cc_ref, sc_ref, qn_ref, kn_ref,
                      qta_ref, ka_ref, vta_ref, qtb_ref, kb_ref, vtb_ref):
    h = _rms(x_ref[...], g_ref[...]).astype(BF16)

    def proj_t(lo, hi):
        return lax.dot_general(wt_ref[lo:hi, :], h, _NT, preferred_element_type=F32)

    ca, sa = ca_ref[...], sa_ref[...]
    cr, sr = cr_ref[...], sr_ref[...]
    cc, sc = cc_ref[...], sc_ref[...]

    def rope_a(blk):
        x1, x2 = blk[0:8], blk[8:16]
        return jnp.concatenate([x1 * ca - x2 * sa, x2 * ca + x1 * sa, blk[16:]], axis=0)

    def norm_rope_b(blk, gain):
        n = blk * lax.rsqrt(jnp.mean(blk * blk, axis=0, keepdims=True) + EPS) * gain
        r1, r2, c1, c2 = n[0:16], n[16:32], n[32:48], n[48:64]
        return jnp.concatenate([r1 * cr - r2 * sr, r2 * cr + r1 * sr,
                                c1 * cc - c2 * sc, c2 * cc + c1 * sc], axis=0)

    o = 0
    aq = proj_t(o, o + A_QK)
    qta_ref[0] = (jnp.concatenate([rope_a(aq[m * 64:(m + 1) * 64]) for m in range(2 * A_HEADS)], axis=0)
                  * Q_SCALE).astype(BF16)
    o += A_QK
    ak = proj_t(o, o + A_QK)
    for hh in range(A_HEADS):
        kt = jnp.concatenate([rope_a(ak[(2 * hh + m) * 64:(2 * hh + m + 1) * 64]) for m in range(2)], axis=0)
        ka_ref[0, hh] = kt.T.astype(BF16)
    o += A_QK
    vta_ref[0] = proj_t(o, o + A_V).astype(BF16)
    o += A_V
    bq = proj_t(o, o + B_Q)
    qn = qn_ref[...]
    qtb_ref[0] = (jnp.concatenate([norm_rope_b(bq[m * 64:(m + 1) * 64], qn) for m in range(B_Q_HEADS)], axis=0)
                  * Q_SCALE).astype(BF16)
    o += B_Q
    bk = proj_t(o, o + B_KV)
    kn = kn_ref[...]
    kbt = jnp.concatenate([norm_rope_b(bk[m * 64:(m + 1) * 64], kn) for m in range(B_KV_HEADS)], axis=0)
    kb_ref[0] = kbt.T.astype(BF16)
    o += B_KV
    bv = proj_t(o, o + B_KV)
    pad = jnp.concatenate([jnp.ones((8, TM), F32), jnp.zeros((VB_ROWS - HEAD_DIM - 8, TM), F32)], axis=0)
    for gg in range(B_KV_HEADS):
        vtb_ref[0, gg] = jnp.concatenate([bv[gg * 64:(gg + 1) * 64], pad], axis=0).astype(BF16)


def _proj_even(xf, g, wt, tabs, qn, kn, batch, seq):
    t = xf.shape[0]
    ns = seq // TM
    tok = lambda i: (i // ns, 0, i % ns)
    tab = lambda rows: pl.BlockSpec((rows, TM), lambda i: (0, i % ns))
    out_shape = (
        jax.ShapeDtypeStruct((batch, A_QK, seq), BF16),
        jax.ShapeDtypeStruct((batch, A_HEADS, seq, 128), BF16),
        jax.ShapeDtypeStruct((batch, A_V, seq), BF16),
        jax.ShapeDtypeStruct((batch, B_Q, seq), BF16),
        jax.ShapeDtypeStruct((batch, seq, B_KV), BF16),
        jax.ShapeDtypeStruct((batch, B_KV_HEADS, VB_ROWS, seq), BF16),
    )
    return pl.pallas_call(
        _proj_even_kernel,
        grid=(t // TM,),
        in_specs=[pl.BlockSpec((TM, D_MODEL), lambda i: (i, 0)),
                  _resident((1, D_MODEL)),
                  _resident((EVEN_IN, D_MODEL)),
                  tab(8), tab(8), tab(16), tab(16), tab(16), tab(16),
                  _resident((HEAD_DIM, 1)), _resident((HEAD_DIM, 1))],
        out_specs=(pl.BlockSpec((1, A_QK, TM), tok),
                   pl.BlockSpec((1, A_HEADS, TM, 128), lambda i: (i // ns, 0, i % ns, 0)),
                   pl.BlockSpec((1, A_V, TM), tok),
                   pl.BlockSpec((1, B_Q, TM), tok),
                   pl.BlockSpec((1, TM, B_KV), lambda i: (i // ns, i % ns, 0)),
                   pl.BlockSpec((1, B_KV_HEADS, VB_ROWS, TM), lambda i: (i // ns, 0, 0, i % ns))),
        out_shape=out_shape,
        compiler_params=_params(1),
        name="proj_even",
    )(xf, g, wt, *tabs, qn, kn)


def _flash_tiles(load_w, scores_fn, values_fn, finalize_fn, n_tiles, n_blocks, w_scr, s_scr, m_scr, l_scr, acc_scr):
    n_slots = s_scr.shape[0]
    assert n_blocks % n_slots == 0

    def reset():
        m_scr[...] = jnp.full(m_scr.shape, -jnp.inf, F32)
        acc_scr[...] = jnp.zeros(acc_scr.shape, F32)
        if l_scr is not None:
            l_scr[...] = jnp.zeros(l_scr.shape, F32)

    def produce(j, slot, w):
        s_t = scores_fn(j, w)
        s_scr[slot] = s_t
        return jnp.max(s_t, axis=0, keepdims=True)

    def consume(j, slot, blk_max):
        m_old = m_scr[...]
        m_new = jnp.maximum(m_old, blk_max)
        alpha = jnp.exp2(m_old - m_new)
        p = jnp.exp2(s_scr[slot] - m_new)
        if l_scr is not None:
            l_scr[...] = alpha * l_scr[...] + jnp.sum(p, axis=0, keepdims=True)
        acc_scr[...] = alpha * acc_scr[...] + _dot(values_fn(j), p.astype(BF16))
        m_scr[...] = m_new

    def body(t, blk_max):
        cur = t % 2
        for u in range(n_blocks):
            if u + 1 < n_blocks:
                nxt = produce(u + 1, (u + 1) % n_slots, w_scr[cur])
            else:
                w_next = load_w(jnp.minimum(t + 1, n_tiles - 1))
                w_scr[1 - cur] = w_next
                nxt = produce(0, 0, w_next)
            consume(u, u % n_slots, blk_max)
            blk_max = nxt
        finalize_fn(t)
        reset()
        return blk_max

    reset()
    w_scr[0] = load_w(0)
    lax.fori_loop(0, n_tiles, body, produce(0, 0, w_scr[0]))


def _attn_a_kernel(q_ref, k_ref, v_ref, lq1_ref, lk1_ref, lq2_ref, lk2_ref, sub_ref, o_ref,
                   w_scr, s_scr, m_scr, l_scr, acc_scr, *, lam_init, seq):
    tile = lambda t: pl.ds(pl.multiple_of(t * TQ_A, TQ_A), TQ_A)
    blk = lambda j: pl.ds(j * TK_A, TK_A)

    def load_w(t):
        qt = q_ref[0, :, tile(t)].astype(F32)
        row = lax.broadcasted_iota(jnp.int32, qt.shape, 0)
        zero = jnp.zeros_like(qt)
        return jnp.concatenate([jnp.where(row < HEAD_DIM, qt, zero), jnp.where(row >= HEAD_DIM, qt, zero)],
                               axis=1).astype(BF16)

    def finalize(t):
        o = acc_scr[...] / l_scr[...]
        lam = (jnp.exp(jnp.sum(lq1_ref[...] * lk1_ref[...], axis=-1, keepdims=True))
               - jnp.exp(jnp.sum(lq2_ref[...] * lk2_ref[...], axis=-1, keepdims=True)) + lam_init)
        d = o[:, :TQ_A] - lam * o[:, TQ_A:]
        y = d * lax.rsqrt(jnp.mean(d * d, axis=0, keepdims=True) + EPS) * sub_ref[...]
        o_ref[0, tile(t), :] = (y * (1.0 - lam_init)).T.astype(BF16)

    _flash_tiles(load_w, lambda j, w: _dot(k_ref[0, 0, blk(j), :], w),
                 lambda j: v_ref[0, :, blk(j)], finalize, seq // TQ_A, seq // TK_A,
                 w_scr, s_scr, m_scr, l_scr, acc_scr)


def _attn_a(qta, ka, vta, lq1, lk1, lq2, lk2, subln, lam_init):
    batch, _, seq = qta.shape
    vec = _resident((1, HEAD_DIM))
    n = 2 * TQ_A
    return pl.pallas_call(
        functools.partial(_attn_a_kernel, lam_init=lam_init, seq=seq),
        grid=(batch, A_HEADS),
        in_specs=[pl.BlockSpec((1, 128, seq), lambda b, h: (b, h, 0)),
                  pl.BlockSpec((1, 1, seq, 128), lambda b, h: (b, h, 0, 0)),
                  pl.BlockSpec((1, 128, seq), lambda b, h: (b, h, 0)),
                  vec, vec, vec, vec,
                  _resident((128, 1))],
        out_specs=pl.BlockSpec((1, seq, 128), lambda b, h: (b, 0, h)),
        out_shape=jax.ShapeDtypeStruct((batch, seq, A_V), BF16),
        scratch_shapes=[pltpu.VMEM((2, 128, n), BF16), pltpu.VMEM((SCORE_SLOTS, TK_A, n), F32),
                        pltpu.VMEM((1, n), F32), pltpu.VMEM((1, n), F32), pltpu.VMEM((128, n), F32)],
        compiler_params=_params(2),
        name="attn_diff",
    )(qta, ka, vta, lq1, lk1, lq2, lk2, subln)


def _attn_b_kernel(q_ref, k_ref, v_ref, o_ref, w_scr, s_scr, m_scr, acc_scr, *, seq):
    g = pl.program_id(1)
    rep = B_Q_HEADS // B_KV_HEADS
    tile = lambda t: pl.ds(pl.multiple_of(t * TQ_B, TQ_B), TQ_B)
    blk = lambda j: pl.ds(j * TK_B, TK_B)

    def load_w(t):
        qt = q_ref[0, :, tile(t)].astype(F32)
        wq = jnp.concatenate([qt[r * 64:(r + 1) * 64] for r in range(rep)], axis=1)
        zero = jnp.zeros_like(wq)
        return jnp.concatenate([jnp.where(g == gg, wq, zero) for gg in range(B_KV_HEADS)], axis=0).astype(BF16)

    def finalize(t):
        o = acc_scr[0:HEAD_DIM, :] / acc_scr[HEAD_DIM:HEAD_DIM + 1, :]
        ot = jnp.concatenate([o[:, r * TQ_B:(r + 1) * TQ_B] for r in range(rep)], axis=0)
        o_ref[0, tile(t), :] = ot.T.astype(BF16)

    _flash_tiles(load_w, lambda j, w: _dot(k_ref[0, blk(j), :], w),
                 lambda j: v_ref[0, 0, :, blk(j)], finalize, seq // TQ_B, seq // TK_B,
                 w_scr, s_scr, m_scr, None, acc_scr)


def _attn_b(qtb, kb, vtb):
    batch, _, seq = qtb.shape
    rep = B_Q_HEADS // B_KV_HEADS
    n = rep * TQ_B
    return pl.pallas_call(
        functools.partial(_attn_b_kernel, seq=seq),
        grid=(batch, B_KV_HEADS),
        in_specs=[pl.BlockSpec((1, rep * HEAD_DIM, seq), lambda b, g: (b, g, 0)),
                  pl.BlockSpec((1, seq, B_KV), lambda b, g: (b, 0, 0)),
                  pl.BlockSpec((1, 1, VB_ROWS, seq), lambda b, g: (b, g, 0, 0))],
        out_specs=pl.BlockSpec((1, seq, rep * HEAD_DIM), lambda b, g: (b, 0, g)),
        out_shape=jax.ShapeDtypeStruct((batch, seq, B_Q), BF16),
        scratch_shapes=[pltpu.VMEM((2, B_KV, n), BF16), pltpu.VMEM((SCORE_SLOTS, TK_B, n), F32),
                        pltpu.VMEM((1, n), F32), pltpu.VMEM((VB_ROWS, n), F32)],
        compiler_params=_params(2),
        name="attn_gqa",
    )(qtb, kb, vtb)


def _proj_odd_kernel(x_ref, g_ref, wqt_ref, wk_ref, wvt_ref, qt_ref, k_ref, vt_ref):
    h = _rms(x_ref[...], g_ref[...]).astype(BF16)
    qt_ref[0] = (lax.dot_general(wqt_ref[...], h, _NT, preferred_element_type=F32) * Q_SCALE).astype(BF16)
    k_ref[0] = _dot(h, wk_ref[...]).astype(BF16)
    vt_ref[0] = lax.dot_general(wvt_ref[...], h, _NT, preferred_element_type=F32).astype(BF16)


def _proj_odd(xf, g, wqt, wk, wvt, batch, seq):
    t = xf.shape[0]
    ns = seq // TM
    tok_t = pl.BlockSpec((1, C_MIX, TM), lambda i: (i // ns, 0, i % ns))
    sq = _resident((C_MIX, D_MODEL))
    return pl.pallas_call(
        _proj_odd_kernel,
        grid=(t // TM,),
        in_specs=[pl.BlockSpec((TM, D_MODEL), lambda i: (i, 0)), _resident((1, D_MODEL)),
                  sq, _resident((D_MODEL, C_MIX)), sq],
        out_specs=(tok_t, pl.BlockSpec((1, TM, C_MIX), lambda i: (i // ns, i % ns, 0)), tok_t),
        out_shape=(jax.ShapeDtypeStruct((batch, C_MIX, seq), BF16),
                   jax.ShapeDtypeStruct((batch, seq, C_MIX), BF16),
                   jax.ShapeDtypeStruct((batch, C_MIX, seq), BF16)),
        compiler_params=_params(1),
        name="proj_odd",
    )(xf, g, wqt, wk, wvt)


def _na_bias_kernel(tab_ref, o_ref):
    h = pl.program_id(0)
    shp = (GRID_W, 2 * GRID_W)
    kc = lax.broadcasted_iota(jnp.int32, shp, 0)
    lane = lax.broadcasted_iota(jnp.int32, shp, 1)
    qc = lane & (GRID_W - 1)
    cs = jnp.clip(qc - NA_KW // 2, 0, GRID_W - NA_KW)
    col_valid = (kc >= cs) & (kc < cs + NA_KW)
    dci = jnp.clip(kc - qc + NA_KW - 1, 0, 2 * NA_KW - 2)
    neg = jnp.full(shp, -jnp.inf, F32)
    n_dc = 2 * NA_KW - 1
    n_dr = 2 * NA_KH - 1
    planes = []
    for dr in range(n_dr):
        acc = jnp.zeros(shp, F32)
        for dd in range(n_dc):
            acc = jnp.where(dci == dd, tab_ref[h, dr * n_dc + dd], acc)
        planes.append(jnp.where(col_valid, acc * LOG2E, neg))
    for i in range(NA_PT):
        dl, dr_ = i - NA_PT_OFF, i - NA_PT_OFF - 1
        left = planes[dl] if 0 <= dl < n_dr else neg
        right = planes[dr_] if 0 <= dr_ < n_dr else neg
        o_ref[0, i] = jnp.where(lane < GRID_W, left, right)


def _na_bias(rel_bias):
    tab = rel_bias.reshape(C_HEADS, -1)
    return pl.pallas_call(
        _na_bias_kernel,
        grid=(C_HEADS,),
        in_specs=[pl.BlockSpec(memory_space=pltpu.SMEM)],
        out_specs=pl.BlockSpec((1, NA_PT, GRID_W, 2 * GRID_W), lambda h: (h, 0, 0, 0)),
        out_shape=jax.ShapeDtypeStruct((C_HEADS, NA_PT, GRID_W, 2 * GRID_W), F32),
        compiler_params=_params(1),
        name="na_bias_table",
    )(tab)


def _na_window_start(r0, rows):
    return jnp.clip(r0 - NA_KH // 2, 0, rows - NA_WIN)


def _na_mask_kernel(o_ref, *, rows):
    c = pl.program_id(0)
    r0 = jnp.where(c == 0, 0, jnp.where(c == 1, NA_ROWS, rows - NA_ROWS))
    ws = _na_window_start(r0, rows)
    shp = (NA_WIN * GRID_W, NA_ROWS * GRID_W)
    kr = ws + lax.broadcasted_iota(jnp.int32, shp, 0) // GRID_W
    r = r0 + lax.broadcasted_iota(jnp.int32, shp, 1) // GRID_W
    rs = jnp.clip(r - NA_KH // 2, 0, rows - NA_KH)
    o_ref[0] = jnp.where((kr >= rs) & (kr < rs + NA_KH), 0.0, -jnp.inf).astype(F32)


def _na_mask(rows):
    shp = (NA_WIN * GRID_W, NA_ROWS * GRID_W)
    return pl.pallas_call(
        functools.partial(_na_mask_kernel, rows=rows),
        grid=(3,),
        out_specs=pl.BlockSpec((1,) + shp, lambda c: (c, 0, 0)),
        out_shape=jax.ShapeDtypeStruct((3,) + shp, F32),
        compiler_params=_params(1),
        name="na_row_mask",
    )()


def _na_kernel(q_ref, k_ref, v_ref, pt_ref, mask_ref, o_ref, s_scr, *, rows):
    nk = NA_WIN * GRID_W
    nq = NA_ROWS * GRID_W
    n_groups = rows // NA_ROWS
    hrow = lax.broadcasted_iota(jnp.int32, (2 * HEAD_DIM, nq), 0) // HEAD_DIM

    def window(g):
        ws = _na_window_start(g * NA_ROWS, rows)
        return ws, pl.ds(pl.multiple_of(ws * GRID_W, 2 * GRID_W), nk)

    def produce(g, slot):
        ws, win = window(g)
        qt = q_ref[0, :, pl.ds(pl.multiple_of(g * nq, nq), nq)].astype(F32)
        zero = jnp.zeros_like(qt)
        w = jnp.concatenate([jnp.where(hrow == hh, qt, zero) for hh in range(2)], axis=1).astype(BF16)
        s_t = _dot(k_ref[0, win, :], w)
        case = jnp.where(g == 0, 0, jnp.where(g == n_groups - 1, 2, 1))
        for j in range(NA_WIN):
            rsl = slice(j * GRID_W, (j + 1) * GRID_W)
            for hh in range(2):
                for a0 in range(0, NA_ROWS, 2):
                    idx = ws - g * NA_ROWS + (j - a0 + NA_KH - 1 + NA_PT_OFF)
                    msl = slice(a0 * GRID_W, (a0 + 2) * GRID_W)
                    csl = slice(hh * nq + a0 * GRID_W, hh * nq + (a0 + 2) * GRID_W)
                    s_scr[slot, rsl, csl] = s_t[rsl, csl] + pt_ref[hh, idx] + mask_ref[case, rsl, msl]
        return jnp.max(s_scr[slot], axis=0, keepdims=True)

    def consume(g, slot, blk_max):
        _, win = window(g)
        p = jnp.exp2(s_scr[slot] - blk_max)
        l = jnp.sum(p, axis=0, keepdims=True)
        ot = _dot(v_ref[0, :, win], p.astype(BF16)) / l
        o = jnp.concatenate([ot[hh * HEAD_DIM:(hh + 1) * HEAD_DIM, hh * nq:(hh + 1) * nq] for hh in range(2)],
                            axis=0)
        o_ref[0, pl.ds(pl.multiple_of(g * nq, nq), nq), :] = o.T.astype(BF16)

    per_trip = min(NA_GROUPS_PER_TRIP, n_groups)
    assert per_trip % 2 == 0 and n_groups % per_trip == 0

    def body(i, blk_max):
        g = per_trip * i
        for u in range(per_trip):
            nxt = produce(jnp.minimum(g + u + 1, n_groups - 1), (u + 1) % 2)
            consume(g + u, u % 2, blk_max)
            blk_max = nxt
        return blk_max

    lax.fori_loop(0, n_groups // per_trip, body, produce(0, 0))


def _na(qt, k, vt, pt, mask):
    batch, seq, _ = k.shape
    rows = seq // GRID_W
    assert rows % 2 == 0 and rows >= NA_WIN + NA_ROWS
    nq = NA_ROWS * GRID_W
    pairs = C_HEADS // 2
    tr = pl.BlockSpec((1, 2 * HEAD_DIM, seq), lambda b, h: (b, h, 0))
    nat = pl.BlockSpec((1, seq, 2 * HEAD_DIM), lambda b, h: (b, 0, h))
    return pl.pallas_call(
        functools.partial(_na_kernel, rows=rows),
        grid=(batch, pairs),
        in_specs=[tr, nat, tr,
                  pl.BlockSpec((2, NA_PT, GRID_W, 2 * GRID_W), lambda b, h: (h, 0, 0, 0)),
                  _resident((3, NA_WIN * GRID_W, nq))],
        out_specs=nat,
        out_shape=jax.ShapeDtypeStruct((batch, seq, C_MIX), BF16),
        scratch_shapes=[pltpu.VMEM((2, NA_WIN * GRID_W, 2 * nq), F32)],
        compiler_params=_params(2),
        name="attn_nbr",
    )(qt, k, vt, pt, mask)


def _outproj_kernel(*refs):
    n = (len(refs) - 2) // 2
    x_ref, o_ref = refs[0], refs[-1]
    y = x_ref[...]
    for a_ref, w_ref in zip(refs[1:1 + n], refs[1 + n:1 + 2 * n]):
        y = y + _dot(a_ref[...], w_ref[...])
    o_ref[...] = y


def _outproj(xf, mixes, weights):
    t = xf.shape[0]
    tile = lambda c: pl.BlockSpec((TM, c), lambda i: (i, 0))
    return pl.pallas_call(
        _outproj_kernel,
        grid=(t // TM,),
        in_specs=[tile(D_MODEL)] + [tile(m.shape[1]) for m in mixes] + [_resident(w.shape) for w in weights],
        out_specs=tile(D_MODEL),
        out_shape=jax.ShapeDtypeStruct((t, D_MODEL), F32),
        compiler_params=_params(1),
        name="out_proj",
    )(xf, *mixes, *weights)


def _ffn_kernel(xm_ref, xp_ref, xn_ref, g_ref, wu_ref, cw_ref, cb_ref, wd_ref, pg_ref, wg_ref, p_ref, wp_ref,
                fg_ref, o_ref, h_scr, acc_scr, u_scr, *, tiles_per_seq, final):
    i = pl.program_id(0)
    g = g_ref[...]
    xm = xm_ref[...]
    pos = i % tiles_per_seq
    hp = jnp.where(pos == 0, 0.0, _rms(xp_ref[...], g))
    hn = jnp.where(pos == tiles_per_seq - 1, 0.0, _rms(xn_ref[...], g))
    h_scr[...] = jnp.concatenate([hp, _rms(xm, g), hn], axis=0).astype(BF16)
    acc_scr[...] = jnp.zeros(acc_scr.shape, F32)

    def conv(u, col):
        cw = cw_ref[:, pl.ds(col, TF)]
        rows = u.shape[0]
        y = (pltpu.roll(u, 1, 0) * cw[0:1] + u * cw[1:2] + pltpu.roll(u, rows - 1, 0) * cw[2:3]
             + cb_ref[:, pl.ds(col, TF)])
        return y[HALO:HALO + TM]

    def up(c, slot):
        h = h_scr[...]
        u_scr[slot, 0] = _dot(h, wu_ref[:, pl.ds(c * TF, TF)])
        u_scr[slot, 1] = _dot(h, wu_ref[:, pl.ds(D_FF + c * TF, TF)])

    def act_down(c, slot):
        gate = conv(u_scr[slot, 0], c * TF)
        val = conv(u_scr[slot, 1], D_FF + c * TF)
        act = gate * jax.nn.sigmoid(gate) * val
        acc_scr[...] += _dot(act.astype(BF16), wd_ref[pl.ds(c * TF, TF), :])

    n_chunks = D_FF // TF
    n_slots = u_scr.shape[0]
    for c in range(min(n_slots - 1, n_chunks)):
        up(c, c)
    for c in range(n_chunks):
        ahead = c + n_slots - 1
        if ahead < n_chunks:
            up(ahead, ahead % n_slots)
        act_down(c, c % n_slots)

    x2 = xm + acc_scr[...]
    gate = jax.nn.sigmoid(_dot(_rms(x2, pg_ref[...]).astype(BF16), wg_ref[...]))
    x3 = x2 + gate * _dot(p_ref[...].astype(BF16), wp_ref[...])
    if final:
        x3 = _rms(x3, fg_ref[...])
    o_ref[...] = x3


def _ffn(xf, g, wu, cw, cb, wd, pg, wg, p, wp, fg, seq, final):
    t = xf.shape[0]
    tps = seq // TM
    nb = TM // HALO
    last = t // HALO - 1
    return pl.pallas_call(
        functools.partial(_ffn_kernel, tiles_per_seq=tps, final=final),
        grid=(t // TM,),
        in_specs=[pl.BlockSpec((TM, D_MODEL), lambda i: (i, 0)),
                  pl.BlockSpec((HALO, D_MODEL), lambda i: (jnp.maximum(i * nb - 1, 0), 0)),
                  pl.BlockSpec((HALO, D_MODEL), lambda i: (jnp.minimum((i + 1) * nb, last), 0)),
                  _resident((1, D_MODEL)),
                  _resident((D_MODEL, 2 * D_FF)),
                  _resident((3, 2 * D_FF)), _resident((1, 2 * D_FF)),
                  _resident((D_FF, D_MODEL)),
                  _resident((1, D_MODEL)),
                  _resident((D_MODEL, D_MODEL)),
                  pl.BlockSpec((TM, PLE_DIM), lambda i: (i, 0)),
                  _resident((PLE_DIM, D_MODEL)),
                  _resident((1, D_MODEL))],
        out_specs=pl.BlockSpec((TM, D_MODEL), lambda i: (i, 0)),
        out_shape=jax.ShapeDtypeStruct((t, D_MODEL), F32),
        scratch_shapes=[pltpu.VMEM((TM + 2 * HALO, D_MODEL), BF16), pltpu.VMEM((TM, D_MODEL), F32),
                        pltpu.VMEM((FFN_SLOTS, 2, TM + 2 * HALO, TF), F32)],
        compiler_params=_params(1),
        name="convffn_ple",
    )(xf, xf, xf, g, wu, cw, cb, wd, pg, wg, p, wp, fg)


def _rope_tables(seq):
    pos = jnp.arange(seq, dtype=jnp.int32)

    def tab(p, half, theta):
        inv = theta ** (-jnp.arange(half, dtype=F32) / half)
        ang = inv[:, None] * p.astype(F32)[None, :]
        return jnp.cos(ang), jnp.sin(ang)

    ca, sa = tab(pos, HEAD_DIM // 8, ROPE_THETA)
    cr, sr = tab(pos // GRID_W, HEAD_DIM // 4, AXIAL_THETA)
    cc, sc = tab(pos % GRID_W, HEAD_DIM // 4, AXIAL_THETA)
    return ca, sa, cr, sr, cc, sc


def kernel(x, p, attn_norm, w_in_ab, lambda_q1, lambda_k1, lambda_q2, lambda_k2, a_subln, b_q_norm, b_k_norm,
           w_out_ab, w_in_c, c_rel_bias, w_out_c, ffn_norm, w_ffn_up, ffn_conv_w, ffn_conv_b, w_ffn_down,
           ple_norm, w_ple_gate, w_ple_proj, final_norm):
    batch, seq, _ = x.shape
    depth = attn_norm.shape[0]
    t = batch * seq
    xf = x.reshape(t, D_MODEL)
    tabs = _rope_tables(seq)
    row = lambda v: v.reshape(1, -1)
    colv = lambda v: v.reshape(-1, 1)
    for i in range(depth):
        j = i // 2
        if i % 2 == 0:
            lam_init = 0.8 - 0.6 * math.exp(-0.3 * i)
            wt = w_in_ab[j].T.astype(BF16)
            qta, ka, vta, qtb, kb, vtb = _proj_even(xf, row(attn_norm[i]), wt, tabs, colv(b_q_norm[j]),
                                                    colv(b_k_norm[j]), batch, seq)
            mix_a = _attn_a(qta, ka, vta, row(lambda_q1[j]), row(lambda_k1[j]), row(lambda_q2[j]),
                            row(lambda_k2[j]), colv(a_subln[j]), lam_init)
            mix_b = _attn_b(qtb, kb, vtb)
            w_out = w_out_ab[j].astype(BF16)
            x1 = _outproj(xf, [mix_a.reshape(t, A_V), mix_b.reshape(t, B_Q)], [w_out[:A_V], w_out[A_V:]])
        else:
            w_c = w_in_c[j].astype(BF16)
            qt, k, vt = _proj_odd(xf, row(attn_norm[i]), w_c[:, :C_MIX].T, w_c[:, C_MIX:2 * C_MIX],
                                  w_c[:, 2 * C_MIX:].T, batch, seq)
            o = _na(qt, k, vt, _na_bias(c_rel_bias[j]), _na_mask(seq // GRID_W))
            x1 = _outproj(xf, [o.reshape(t, C_MIX)], [w_out_c[j].astype(BF16)])
        xf = _ffn(x1, row(ffn_norm[i]), w_ffn_up[i].astype(BF16), ffn_conv_w[i], row(ffn_conv_b[i]),
                  w_ffn_down[i].astype(BF16), row(ple_norm[i]), w_ple_gate[i].astype(BF16),
                  p[i].reshape(t, PLE_DIM), w_ple_proj[i].astype(BF16), row(final_norm), seq,
                  final=(i == depth - 1))
    return xf.reshape(batch, seq, D_MODEL)
```

```python
import functools
import math

import jax
import jax.numpy as jnp
from jax import lax
from jax.experimental import pallas as pl
from jax.experimental.pallas import tpu as pltpu

F32 = jnp.float32
BF16 = jnp.bfloat16

D_MODEL = 1024
HEAD_DIM = 64
A_HEADS = 4
B_Q_HEADS = 8
B_KV_HEADS = 2
C_HEADS = 16
ROPE_THETA = 500000.0
AXIAL_THETA = 10000.0
GRID_W = 64
NA_KH = 8
NA_KW = 16
D_FF = 2816
PLE_DIM = 256
EPS = 1e-6
SCALE = HEAD_DIM ** -0.5
LOG2E = math.log2(math.e)
Q_SCALE = SCALE * LOG2E

A_QK = A_HEADS * 2 * HEAD_DIM
A_V = A_HEADS * 2 * HEAD_DIM
B_Q = B_Q_HEADS * HEAD_DIM
B_KV = B_KV_HEADS * HEAD_DIM
EVEN_IN = 2 * A_QK + A_V + B_Q + 2 * B_KV
C_MIX = C_HEADS * HEAD_DIM

TM = 1024
TQ_A = 512
TQ_B = 256
TK_A = 512
TK_B = 256
VB_ROWS = HEAD_DIM + 16
SCORE_SLOTS = 2
TM_FFN = 256
TF = 256
FFN_SLOTS = 4
HALO = 8
NA_ROWS = 4
NA_WIN = NA_ROWS + NA_KH
NA_PT_OFF = NA_WIN - NA_KH - 1
NA_PT = NA_WIN + NA_KH - 1 + NA_PT_OFF
NA_GROUPS_PER_TRIP = 4
VMEM_LIMIT = 52 * 1024 * 1024

_NT = (((1,), (1,)), ((), ()))


def _dot(a, b):
    return jnp.dot(a, b, preferred_element_type=F32)


def _rms(xf, g):
    return xf * lax.rsqrt(jnp.mean(xf * xf, axis=-1, keepdims=True) + EPS) * g


def _params(n_axes, limit=VMEM_LIMIT):
    return pltpu.CompilerParams(dimension_semantics=("arbitrary",) * n_axes, vmem_limit_bytes=limit)


def _resident(shape):
    nd = len(shape)
    return pl.BlockSpec(shape, lambda *_: (0,) * nd, pipeline_mode=pl.Buffered(1))


def _proj_even_kernel(x_ref, g_ref, wt_ref, ca_ref, sa_ref, cr_ref, sr_ref, cc_ref, sc_ref, qn_ref, kn_ref,
                      qta_ref, ka_ref, vta_ref, qtb_ref, kb_ref, vtb_ref):
    h = _rms(x_ref[...], g_ref[...]).astype(BF16)

    def proj_t(lo, hi):
        return lax.dot_general(wt_ref[lo:hi, :], h, _NT, preferred_element_type=F32)

    ca, sa = ca_ref[...], sa_ref[...]
    cr, sr = cr_ref[...], sr_ref[...]
    cc, sc = cc_ref[...], sc_ref[...]

    def rope_a(blk):
        x1, x2 = blk[0:8], blk[8:16]
        return jnp.concatenate([x1 * ca - x2 * sa, x2 * ca + x1 * sa, blk[16:]], axis=0)

    def norm_rope_b(blk, gain):
        n = blk * lax.rsqrt(jnp.mean(blk * blk, axis=0, keepdims=True) + EPS) * gain
        r1, r2, c1, c2 = n[0:16], n[16:32], n[32:48], n[48:64]
        return jnp.concatenate([r1 * cr - r2 * sr, r2 * cr + r1 * sr,
                                c1 * cc - c2 * sc, c2 * cc + c1 * sc], axis=0)

    o = 0
    aq = proj_t(o, o + A_QK)
    qta_ref[0] = (jnp.concatenate([rope_a(aq[m * 64:(m + 1) * 64]) for m in range(2 * A_HEADS)], axis=0)
                  * Q_SCALE).astype(BF16)
    o += A_QK
    ak = proj_t(o, o + A_QK)
    for hh in range(A_HEADS):
        kt = jnp.concatenate([rope_a(ak[(2 * hh + m) * 64:(2 * hh + m + 1) * 64]) for m in range(2)], axis=0)
        ka_ref[0, hh] = kt.T.astype(BF16)
    o += A_QK
    vta_ref[0] = proj_t(o, o + A_V).astype(BF16)
    o += A_V
    bq = proj_t(o, o + B_Q)
    qn = qn_ref[...]
    qtb_ref[0] = (jnp.concatenate([norm_rope_b(bq[m * 64:(m + 1) * 64], qn) for m in range(B_Q_HEADS)], axis=0)
                  * Q_SCALE).astype(BF16)
    o += B_Q
    bk = proj_t(o, o + B_KV)
    kn = kn_ref[...]
    kbt = jnp.concatenate([norm_rope_b(bk[m * 64:(m + 1) * 64], kn) for m in range(B_KV_HEADS)], axis=0)
    kb_ref[0] = kbt.T.astype(BF16)
    o += B_KV
    bv = proj_t(o, o + B_KV)
    pad = jnp.concatenate([jnp.ones((8, TM), F32), jnp.zeros((VB_ROWS - HEAD_DIM - 8, TM), F32)], axis=0)
    for gg in range(B_KV_HEADS):
        vtb_ref[0, gg] = jnp.concatenate([bv[gg * 64:(gg + 1) * 64], pad], axis=0).astype(BF16)


def _proj_even(xf, g, wt, tabs, qn, kn, batch, seq):
    t = xf.shape[0]
    ns = seq // TM
    tok = lambda i: (i // ns, 0, i % ns)
    tab = lambda rows: pl.BlockSpec((rows, TM), lambda i: (0, i % ns))
    out_shape = (
        jax.ShapeDtypeStruct((batch, A_QK, seq), BF16),
        jax.ShapeDtypeStruct((batch, A_HEADS, seq, 128), BF16),
        jax.ShapeDtypeStruct((batch, A_V, seq), BF16),
        jax.ShapeDtypeStruct((batch, B_Q, seq), BF16),
        jax.ShapeDtypeStruct((batch, seq, B_KV), BF16),
        jax.ShapeDtypeStruct((batch, B_KV_HEADS, VB_ROWS, seq), BF16),
    )
    return pl.pallas_call(
        _proj_even_kernel,
        grid=(t // TM,),
        in_specs=[pl.BlockSpec((TM, D_MODEL), lambda i: (i, 0)),
                  _resident((1, D_MODEL)),
                  _resident((EVEN_IN, D_MODEL)),
                  tab(8), tab(8), tab(16), tab(16), tab(16), tab(16),
                  _resident((HEAD_DIM, 1)), _resident((HEAD_DIM, 1))],
        out_specs=(pl.BlockSpec((1, A_QK, TM), tok),
                   pl.BlockSpec((1, A_HEADS, TM, 128), lambda i: (i // ns, 0, i % ns, 0)),
                   pl.BlockSpec((1, A_V, TM), tok),
                   pl.BlockSpec((1, B_Q, TM), tok),
                   pl.BlockSpec((1, TM, B_KV), lambda i: (i // ns, i % ns, 0)),
                   pl.BlockSpec((1, B_KV_HEADS, VB_ROWS, TM), lambda i: (i // ns, 0, 0, i % ns))),
        out_shape=out_shape,
        compiler_params=_params(1),
        name="proj_even",
    )(xf, g, wt, *tabs, qn, kn)


def _flash_tiles(load_w, scores_fn, values_fn, finalize_fn, n_tiles, n_blocks, w_scr, s_scr, m_scr, l_scr, acc_scr):
    n_slots = s_scr.shape[0]
    assert n_blocks % n_slots == 0

    def reset():
        m_scr[...] = jnp.full(m_scr.shape, -jnp.inf, F32)
        acc_scr[...] = jnp.zeros(acc_scr.shape, F32)
        if l_scr is not None:
            l_scr[...] = jnp.zeros(l_scr.shape, F32)

    def produce(j, slot, w):
        s_t = scores_fn(j, w)
        s_scr[slot] = s_t
        return jnp.max(s_t, axis=0, keepdims=True)

    def consume(j, slot, blk_max):
        m_old = m_scr[...]
        m_new = jnp.maximum(m_old, blk_max)
        alpha = jnp.exp2(m_old - m_new)
        p = jnp.exp2(s_scr[slot] - m_new)
        if l_scr is not None:
            l_scr[...] = alpha * l_scr[...] + jnp.sum(p, axis=0, keepdims=True)
        acc_scr[...] = alpha * acc_scr[...] + _dot(values_fn(j), p.astype(BF16))
        m_scr[...] = m_new

    def body(t, blk_max):
        cur = t % 2
        for u in range(n_blocks):
            if u + 1 < n_blocks:
                nxt = produce(u + 1, (u + 1) % n_slots, w_scr[cur])
            else:
                w_next = load_w(jnp.minimum(t + 1, n_tiles - 1))
                w_scr[1 - cur] = w_next
                nxt = produce(0, 0, w_next)
            consume(u, u % n_slots, blk_max)
            blk_max = nxt
        finalize_fn(t)
        reset()
        return blk_max

    reset()
    w_scr[0] = load_w(0)
    lax.fori_loop(0, n_tiles, body, produce(0, 0, w_scr[0]))


def _attn_a_kernel(q_ref, k_ref, v_ref, lq1_ref, lk1_ref, lq2_ref, lk2_ref, sub_ref, o_ref,
                   w_scr, s_scr, m_scr, l_scr, acc_scr, *, lam_init, seq):
    tile = lambda t: pl.ds(pl.multiple_of(t * TQ_A, TQ_A), TQ_A)
    blk = lambda j: pl.ds(j * TK_A, TK_A)

    def load_w(t):
        qt = q_ref[0, :, tile(t)].astype(F32)
        row = lax.broadcasted_iota(jnp.int32, qt.shape, 0)
        zero = jnp.zeros_like(qt)
        return jnp.concatenate([jnp.where(row < HEAD_DIM, qt, zero), jnp.where(row >= HEAD_DIM, qt, zero)],
                               axis=1).astype(BF16)

    def finalize(t):
        o = acc_scr[...] / l_scr[...]
        lam = (jnp.exp(jnp.sum(lq1_ref[...] * lk1_ref[...], axis=-1, keepdims=True))
               - jnp.exp(jnp.sum(lq2_ref[...] * lk2_ref[...], axis=-1, keepdims=True)) + lam_init)
        d = o[:, :TQ_A] - lam * o[:, TQ_A:]
        y = d * lax.rsqrt(jnp.mean(d * d, axis=0, keepdims=True) + EPS) * sub_ref[...]
        o_ref[0, tile(t), :] = (y * (1.0 - lam_init)).T.astype(BF16)

    _flash_tiles(load_w, lambda j, w: _dot(k_ref[0, 0, blk(j), :], w),
                 lambda j: v_ref[0, :, blk(j)], finalize, seq // TQ_A, seq // TK_A,
                 w_scr, s_scr, m_scr, l_scr, acc_scr)


def _attn_a(qta, ka, vta, lq1, lk1, lq2, lk2, subln, lam_init):
    batch, _, seq = qta.shape
    vec = _resident((1, HEAD_DIM))
    n = 2 * TQ_A
    return pl.pallas_call(
        functools.partial(_attn_a_kernel, lam_init=lam_init, seq=seq),
        grid=(batch, A_HEADS),
        in_specs=[pl.BlockSpec((1, 128, seq), lambda b, h: (b, h, 0)),
                  pl.BlockSpec((1, 1, seq, 128), lambda b, h: (b, h, 0, 0)),
                  pl.BlockSpec((1, 128, seq), lambda b, h: (b, h, 0)),
                  vec, vec, vec, vec,
                  _resident((128, 1))],
        out_specs=pl.BlockSpec((1, seq, 128), lambda b, h: (b, 0, h)),
        out_shape=jax.ShapeDtypeStruct((batch, seq, A_V), BF16),
        scratch_shapes=[pltpu.VMEM((2, 128, n), BF16), pltpu.VMEM((SCORE_SLOTS, TK_A, n), F32),
                        pltpu.VMEM((1, n), F32), pltpu.VMEM((1, n), F32), pltpu.VMEM((128, n), F32)],
        compiler_params=_params(2),
        name="attn_diff",
    )(qta, ka, vta, lq1, lk1, lq2, lk2, subln)


def _attn_b_kernel(q_ref, k_ref, v_ref, o_ref, w_scr, s_scr, m_scr, acc_scr, *, seq):
    g = pl.program_id(1)
    rep = B_Q_HEADS // B_KV_HEADS
    tile = lambda t: pl.ds(pl.multiple_of(t * TQ_B, TQ_B), TQ_B)
    blk = lambda j: pl.ds(j * TK_B, TK_B)

    def load_w(t):
        qt = q_ref[0, :, tile(t)].astype(F32)
        wq = jnp.concatenate([qt[r * 64:(r + 1) * 64] for r in range(rep)], axis=1)
        zero = jnp.zeros_like(wq)
        return jnp.concatenate([jnp.where(g == gg, wq, zero) for gg in range(B_KV_HEADS)], axis=0).astype(BF16)

    def finalize(t):
        o = acc_scr[0:HEAD_DIM, :] / acc_scr[HEAD_DIM:HEAD_DIM + 1, :]
        ot = jnp.concatenate([o[:, r * TQ_B:(r + 1) * TQ_B] for r in range(rep)], axis=0)
        o_ref[0, tile(t), :] = ot.T.astype(BF16)

    _flash_tiles(load_w, lambda j, w: _dot(k_ref[0, blk(j), :], w),
                 lambda j: v_ref[0, 0, :, blk(j)], finalize, seq // TQ_B, seq // TK_B,
                 w_scr, s_scr, m_scr, None, acc_scr)


def _attn_b(qtb, kb, vtb):
    batch, _, seq = qtb.shape
    rep = B_Q_HEADS // B_KV_HEADS
    n = rep * TQ_B
    return pl.pallas_call(
        functools.partial(_attn_b_kernel, seq=seq),
        grid=(batch, B_KV_HEADS),
        in_specs=[pl.BlockSpec((1, rep * HEAD_DIM, seq), lambda b, g: (b, g, 0)),
                  pl.BlockSpec((1, seq, B_KV), lambda b, g: (b, 0, 0)),
                  pl.BlockSpec((1, 1, VB_ROWS, seq), lambda b, g: (b, g, 0, 0))],
        out_specs=pl.BlockSpec((1, seq, rep * HEAD_DIM), lambda b, g: (b, 0, g)),
        out_shape=jax.ShapeDtypeStruct((batch, seq, B_Q), BF16),
        scratch_shapes=[pltpu.VMEM((2, B_KV, n), BF16), pltpu.VMEM((SCORE_SLOTS, TK_B, n), F32),
                        pltpu.VMEM((1, n), F32), pltpu.VMEM((VB_ROWS, n), F32)],
        compiler_params=_params(2),
        name="attn_gqa",
    )(qtb, kb, vtb)


def _proj_odd_kernel(x_ref, g_ref, wqt_ref, wk_ref, wvt_ref, qt_ref, k_ref, vt_ref):
    h = _rms(x_ref[...], g_ref[...]).astype(BF16)
    qt_ref[0] = (lax.dot_general(wqt_ref[...], h, _NT, preferred_element_type=F32) * Q_SCALE).astype(BF16)
    k_ref[0] = _dot(h, wk_ref[...]).astype(BF16)
    vt_ref[0] = lax.dot_general(wvt_ref[...], h, _NT, preferred_element_type=F32).astype(BF16)


def _proj_odd(xf, g, wqt, wk, wvt, batch, seq):
    t = xf.shape[0]
    ns = seq // TM
    tok_t = pl.BlockSpec((1, C_MIX, TM), lambda i: (i // ns, 0, i % ns))
    sq = _resident((C_MIX, D_MODEL))
    return pl.pallas_call(
        _proj_odd_kernel,
        grid=(t // TM,),
        in_specs=[pl.BlockSpec((TM, D_MODEL), lambda i: (i, 0)), _resident((1, D_MODEL)),
                  sq, _resident((D_MODEL, C_MIX)), sq],
        out_specs=(tok_t, pl.BlockSpec((1, TM, C_MIX), lambda i: (i // ns, i % ns, 0)), tok_t),
        out_shape=(jax.ShapeDtypeStruct((batch, C_MIX, seq), BF16),
                   jax.ShapeDtypeStruct((batch, seq, C_MIX), BF16),
                   jax.ShapeDtypeStruct((batch, C_MIX, seq), BF16)),
        compiler_params=_params(1),
        name="proj_odd",
    )(xf, g, wqt, wk, wvt)


def _na_bias_kernel(tab_ref, o_ref):
    h = pl.program_id(0)
    shp = (GRID_W, 2 * GRID_W)
    kc = lax.broadcasted_iota(jnp.int32, shp, 0)
    lane = lax.broadcasted_iota(jnp.int32, shp, 1)
    qc = lane & (GRID_W - 1)
    cs = jnp.clip(qc - NA_KW // 2, 0, GRID_W - NA_KW)
    col_valid = (kc >= cs) & (kc < cs + NA_KW)
    dci = jnp.clip(kc - qc + NA_KW - 1, 0, 2 * NA_KW - 2)
    neg = jnp.full(shp, -jnp.inf, F32)
    n_dc = 2 * NA_KW - 1
    n_dr = 2 * NA_KH - 1
    planes = []
    for dr in range(n_dr):
        acc = jnp.zeros(shp, F32)
        for dd in range(n_dc):
            acc = jnp.where(dci == dd, tab_ref[h, dr * n_dc + dd], acc)
        planes.append(jnp.where(col_valid, acc * LOG2E, neg))
    for i in range(NA_PT):
        dl, dr_ = i - NA_PT_OFF, i - NA_PT_OFF - 1
        left = planes[dl] if 0 <= dl < n_dr else neg
        right = planes[dr_] if 0 <= dr_ < n_dr else neg
        o_ref[0, i] = jnp.where(lane < GRID_W, left, right)


def _na_bias(rel_bias):
    tab = rel_bias.reshape(C_HEADS, -1)
    return pl.pallas_call(
        _na_bias_kernel,
        grid=(C_HEADS,),
        in_specs=[pl.BlockSpec(memory_space=pltpu.SMEM)],
        out_specs=pl.BlockSpec((1, NA_PT, GRID_W, 2 * GRID_W), lambda h: (h, 0, 0, 0)),
        out_shape=jax.ShapeDtypeStruct((C_HEADS, NA_PT, GRID_W, 2 * GRID_W), F32),
        compiler_params=_params(1),
        name="na_bias_table",
    )(tab)


def _na_window_start(r0, rows):
    return jnp.clip(r0 - NA_KH // 2, 0, rows - NA_WIN)


def _na_mask_kernel(o_ref, *, rows):
    c = pl.program_id(0)
    r0 = jnp.where(c == 0, 0, jnp.where(c == 1, NA_ROWS, rows - NA_ROWS))
    ws = _na_window_start(r0, rows)
    shp = (NA_WIN * GRID_W, NA_ROWS * GRID_W)
    kr = ws + lax.broadcasted_iota(jnp.int32, shp, 0) // GRID_W
    r = r0 + lax.broadcasted_iota(jnp.int32, shp, 1) // GRID_W
    rs = jnp.clip(r - NA_KH // 2, 0, rows - NA_KH)
    o_ref[0] = jnp.where((kr >= rs) & (kr < rs + NA_KH), 0.0, -jnp.inf).astype(F32)


def _na_mask(rows):
    shp = (NA_WIN * GRID_W, NA_ROWS * GRID_W)
    return pl.pallas_call(
        functools.partial(_na_mask_kernel, rows=rows),
        grid=(3,),
        out_specs=pl.BlockSpec((1,) + shp, lambda c: (c, 0, 0)),
        out_shape=jax.ShapeDtypeStruct((3,) + shp, F32),
        compiler_params=_params(1),
        name="na_row_mask",
    )()


def _na_kernel(q_ref, k_ref, v_ref, pt_ref, mask_ref, o_ref, s_scr, *, rows):
    nk = NA_WIN * GRID_W
    nq = NA_ROWS * GRID_W
    n_groups = rows // NA_ROWS
    hrow = lax.broadcasted_iota(jnp.int32, (2 * HEAD_DIM, nq), 0) // HEAD_DIM

    def window(g):
        ws = _na_window_start(g * NA_ROWS, rows)
        return ws, pl.ds(pl.multiple_of(ws * GRID_W, 2 * GRID_W), nk)

    def produce(g, slot):
        ws, win = window(g)
        qt = q_ref[0, :, pl.ds(pl.multiple_of(g * nq, nq), nq)].astype(F32)
        zero = jnp.zeros_like(qt)
        w = jnp.concatenate([jnp.where(hrow == hh, qt, zero) for hh in range(2)], axis=1).astype(BF16)
        s_t = _dot(k_ref[0, win, :], w)
        case = jnp.where(g == 0, 0, jnp.where(g == n_groups - 1, 2, 1))
        for j in range(NA_WIN):
            rsl = slice(j * GRID_W, (j + 1) * GRID_W)
            for hh in range(2):
                for a0 in range(0, NA_ROWS, 2):
                    idx = ws - g * NA_ROWS + (j - a0 + NA_KH - 1 + NA_PT_OFF)
                    msl = slice(a0 * GRID_W, (a0 + 2) * GRID_W)
                    csl = slice(hh * nq + a0 * GRID_W, hh * nq + (a0 + 2) * GRID_W)
                    s_scr[slot, rsl, csl] = s_t[rsl, csl] + pt_ref[hh, idx] + mask_ref[case, rsl, msl]
        return jnp.max(s_scr[slot], axis=0, keepdims=True)

    def consume(g, slot, blk_max):
        _, win = window(g)
        p = jnp.exp2(s_scr[slot] - blk_max)
        l = jnp.sum(p, axis=0, keepdims=True)
        ot = _dot(v_ref[0, :, win], p.astype(BF16)) / l
        o = jnp.concatenate([ot[hh * HEAD_DIM:(hh + 1) * HEAD_DIM, hh * nq:(hh + 1) * nq] for hh in range(2)],
                            axis=0)
        o_ref[0, pl.ds(pl.multiple_of(g * nq, nq), nq), :] = o.T.astype(BF16)

    per_trip = min(NA_GROUPS_PER_TRIP, n_groups)
    assert per_trip % 2 == 0 and n_groups % per_trip == 0

    def body(i, blk_max):
        g = per_trip * i
        for u in range(per_trip):
            nxt = produce(jnp.minimum(g + u + 1, n_groups - 1), (u + 1) % 2)
            consume(g + u, u % 2, blk_max)
            blk_max = nxt
        return blk_max

    lax.fori_loop(0, n_groups // per_trip, body, produce(0, 0))


def _na(qt, k, vt, pt, mask):
    batch, seq, _ = k.shape
    rows = seq // GRID_W
    assert rows % 2 == 0 and rows >= NA_WIN + NA_ROWS
    nq = NA_ROWS * GRID_W
    pairs = C_HEADS // 2
    tr = pl.BlockSpec((1, 2 * HEAD_DIM, seq), lambda b, h: (b, h, 0))
    nat = pl.BlockSpec((1, seq, 2 * HEAD_DIM), lambda b, h: (b, 0, h))
    return pl.pallas_call(
        functools.partial(_na_kernel, rows=rows),
        grid=(batch, pairs),
        in_specs=[tr, nat, tr,
                  pl.BlockSpec((2, NA_PT, GRID_W, 2 * GRID_W), lambda b, h: (h, 0, 0, 0)),
                  _resident((3, NA_WIN * GRID_W, nq))],
        out_specs=nat,
        out_shape=jax.ShapeDtypeStruct((batch, seq, C_MIX), BF16),
        scratch_shapes=[pltpu.VMEM((2, NA_WIN * GRID_W, 2 * nq), F32)],
        compiler_params=_params(2),
        name="attn_nbr",
    )(qt, k, vt, pt, mask)


def _outproj_kernel(*refs):
    n = (len(refs) - 2) // 2
    x_ref, o_ref = refs[0], refs[-1]
    y = x_ref[...]
    for a_ref, w_ref in zip(refs[1:1 + n], refs[1 + n:1 + 2 * n]):
        y = y + _dot(a_ref[...], w_ref[...])
    o_ref[...] = y


def _outproj(xf, mixes, weights):
    t = xf.shape[0]
    tile = lambda c: pl.BlockSpec((TM, c), lambda i: (i, 0))
    return pl.pallas_call(
        _outproj_kernel,
        grid=(t // TM,),
        in_specs=[tile(D_MODEL)] + [tile(m.shape[1]) for m in mixes] + [_resident(w.shape) for w in weights],
        out_specs=tile(D_MODEL),
        out_shape=jax.ShapeDtypeStruct((t, D_MODEL), F32),
        compiler_params=_params(1),
        name="out_proj",
    )(xf, *mixes, *weights)


def _ffn_kernel(xm_ref, xp_ref, xn_ref, g_ref, wu_ref, cw_ref, cb_ref, wd_ref, pg_ref, wg_ref, p_ref, wp_ref,
                fg_ref, o_ref, h_scr, acc_scr, u_scr, *, tiles_per_seq, final):
    i = pl.program_id(0)
    g = g_ref[...]
    xm = xm_ref[...]
    pos = i % tiles_per_seq
    hp = jnp.where(pos == 0, 0.0, _rms(xp_ref[...], g))
    hn = jnp.where(pos == tiles_per_seq - 1, 0.0, _rms(xn_ref[...], g))
    h_scr[...] = jnp.concatenate([hp, _rms(xm, g), hn], axis=0).astype(BF16)
    acc_scr[...] = jnp.zeros(acc_scr.shape, F32)

    def conv(u, col):
        cw = cw_ref[:, pl.ds(col, TF)]
        rows = u.shape[0]
        y = (pltpu.roll(u, 1, 0) * cw[0:1] + u * cw[1:2] + pltpu.roll(u, rows - 1, 0) * cw[2:3]
             + cb_ref[:, pl.ds(col, TF)])
        return y[HALO:HALO + TM_FFN]

    def up(c, slot):
        h = h_scr[...]
        u_scr[slot, 0] = _dot(h, wu_ref[:, pl.ds(c * TF, TF)])
        u_scr[slot, 1] = _dot(h, wu_ref[:, pl.ds(D_FF + c * TF, TF)])

    def act_down(c, slot):
        gate = conv(u_scr[slot, 0], c * TF)
        val = conv(u_scr[slot, 1], D_FF + c * TF)
        act = gate * jax.nn.sigmoid(gate) * val
        acc_scr[...] += _dot(act.astype(BF16), wd_ref[pl.ds(c * TF, TF), :])

    n_chunks = D_FF // TF
    n_slots = u_scr.shape[0]
    for c in range(min(n_slots - 1, n_chunks)):
        up(c, c)
    for c in range(n_chunks):
        ahead = c + n_slots - 1
        if ahead < n_chunks:
            up(ahead, ahead % n_slots)
        act_down(c, c % n_slots)

    x2 = xm + acc_scr[...]
    gate = jax.nn.sigmoid(_dot(_rms(x2, pg_ref[...]).astype(BF16), wg_ref[...]))
    x3 = x2 + gate * _dot(p_ref[...].astype(BF16), wp_ref[...])
    if final:
        x3 = _rms(x3, fg_ref[...])
    o_ref[...] = x3


def _ffn(xf, g, wu, cw, cb, wd, pg, wg, p, wp, fg, seq, final):
    t = xf.shape[0]
    tps = seq // TM_FFN
    nb = TM_FFN // HALO
    last = t // HALO - 1
    return pl.pallas_call(
        functools.partial(_ffn_kernel, tiles_per_seq=tps, final=final),
        grid=(t // TM_FFN,),
        in_specs=[pl.BlockSpec((TM_FFN, D_MODEL), lambda i: (i, 0)),
                  pl.BlockSpec((HALO, D_MODEL), lambda i: (jnp.maximum(i * nb - 1, 0), 0)),
                  pl.BlockSpec((HALO, D_MODEL), lambda i: (jnp.minimum((i + 1) * nb, last), 0)),
                  _resident((1, D_MODEL)),
                  _resident((D_MODEL, 2 * D_FF)),
                  _resident((3, 2 * D_FF)), _resident((1, 2 * D_FF)),
                  _resident((D_FF, D_MODEL)),
                  _resident((1, D_MODEL)),
                  _resident((D_MODEL, D_MODEL)),
                  pl.BlockSpec((TM_FFN, PLE_DIM), lambda i: (i, 0)),
                  _resident((PLE_DIM, D_MODEL)),
                  _resident((1, D_MODEL))],
        out_specs=pl.BlockSpec((TM_FFN, D_MODEL), lambda i: (i, 0)),
        out_shape=jax.ShapeDtypeStruct((t, D_MODEL), F32),
        scratch_shapes=[pltpu.VMEM((TM_FFN + 2 * HALO, D_MODEL), BF16), pltpu.VMEM((TM_FFN, D_MODEL), F32),
                        pltpu.VMEM((FFN_SLOTS, 2, TM_FFN + 2 * HALO, TF), F32)],
        compiler_params=_params(1),
        name="convffn_ple",
    )(xf, xf, xf, g, wu, cw, cb, wd, pg, wg, p, wp, fg)


def _rope_tables(seq):
    pos = jnp.arange(seq, dtype=jnp.int32)

    def tab(p, half, theta):
        inv = theta ** (-jnp.arange(half, dtype=F32) / half)
        ang = inv[:, None] * p.astype(F32)[None, :]
        return jnp.cos(ang), jnp.sin(ang)

    ca, sa = tab(pos, HEAD_DIM // 8, ROPE_THETA)
    cr, sr = tab(pos // GRID_W, HEAD_DIM // 4, AXIAL_THETA)
    cc, sc = tab(pos % GRID_W, HEAD_DIM // 4, AXIAL_THETA)
    return ca, sa, cr, sr, cc, sc


def kernel(x, p, attn_norm, w_in_ab, lambda_q1, lambda_k1, lambda_q2, lambda_k2, a_subln, b_q_norm, b_k_norm,
           w_out_ab, w_in_c, c_rel_bias, w_out_c, ffn_norm, w_ffn_up, ffn_conv_w, ffn_conv_b, w_ffn_down,
           ple_norm, w_ple_gate, w_ple_proj, final_norm):
    batch, seq, _ = x.shape
    depth = attn_norm.shape[0]
    t = batch * seq
    xf = x.reshape(t, D_MODEL)
    tabs = _rope_tables(seq)
    row = lambda v: v.reshape(1, -1)
    colv = lambda v: v.reshape(-1, 1)
    for i in range(depth):
        j = i // 2
        if i % 2 == 0:
            lam_init = 0.8 - 0.6 * math.exp(-0.3 * i)
            wt = w_in_ab[j].T.astype(BF16)
            qta, ka, vta, qtb, kb, vtb = _proj_even(xf, row(attn_norm[i]), wt, tabs, colv(b_q_norm[j]),
                                                    colv(b_k_norm[j]), batch, seq)
            mix_a = _attn_a(qta, ka, vta, row(lambda_q1[j]), row(lambda_k1[j]), row(lambda_q2[j]),
                            row(lambda_k2[j]), colv(a_subln[j]), lam_init)
            mix_b = _attn_b(qtb, kb, vtb)
            w_out = w_out_ab[j].astype(BF16)
            x1 = _outproj(xf, [mix_a.reshape(t, A_V), mix_b.reshape(t, B_Q)], [w_out[:A_V], w_out[A_V:]])
        else:
            w_c = w_in_c[j].astype(BF16)
            qt, k, vt = _proj_odd(xf, row(attn_norm[i]), w_c[:, :C_MIX].T, w_c[:, C_MIX:2 * C_MIX],
                                  w_c[:, 2 * C_MIX:].T, batch, seq)
            o = _na(qt, k, vt, _na_bias(c_rel_bias[j]), _na_mask(seq // GRID_W))
            x1 = _outproj(xf, [o.reshape(t, C_MIX)], [w_out_c[j].astype(BF16)])
        xf = _ffn(x1, row(ffn_norm[i]), w_ffn_up[i].astype(BF16), ffn_conv_w[i], row(ffn_conv_b[i]),
                  w_ffn_down[i].astype(BF16), row(ple_norm[i]), w_ple_gate[i].astype(BF16),
                  p[i].reshape(t, PLE_DIM), w_ple_proj[i].astype(BF16), row(final_norm), seq,
                  final=(i == depth - 1))
    return xf.reshape(batch, seq, D_MODEL)
```

```python
import functools
import math

import jax
import jax.numpy as jnp
from jax import lax
from jax.experimental import pallas as pl
from jax.experimental.pallas import tpu as pltpu

F32 = jnp.float32
BF16 = jnp.bfloat16

D_MODEL = 1024
HEAD_DIM = 64
A_HEADS = 4
B_Q_HEADS = 8
B_KV_HEADS = 2
C_HEADS = 16
ROPE_THETA = 500000.0
AXIAL_THETA = 10000.0
GRID_W = 64
NA_KH = 8
NA_KW = 16
D_FF = 2816
PLE_DIM = 256
EPS = 1e-6
SCALE = HEAD_DIM ** -0.5
LOG2E = math.log2(math.e)
Q_SCALE = SCALE * LOG2E

A_QK = A_HEADS * 2 * HEAD_DIM
A_V = A_HEADS * 2 * HEAD_DIM
B_Q = B_Q_HEADS * HEAD_DIM
B_KV = B_KV_HEADS * HEAD_DIM
EVEN_IN = 2 * A_QK + A_V + B_Q + 2 * B_KV
C_MIX = C_HEADS * HEAD_DIM

TM = 1024
TQ_A = 512
TQ_B = 256
TK_A = 512
TK_B = 256
VB_ROWS = HEAD_DIM + 16
SCORE_SLOTS = 2
TM_FFN = 256
TF = 256
FFN_SLOTS = 4
HALO = 8
NA_ROWS = 4
NA_WIN = NA_ROWS + NA_KH
NA_PT_OFF = NA_WIN - NA_KH - 1
NA_PT = NA_WIN + NA_KH - 1 + NA_PT_OFF
NA_GROUPS_PER_TRIP = 4
VMEM_LIMIT = 52 * 1024 * 1024

_NT = (((1,), (1,)), ((), ()))


def _dot(a, b):
    return jnp.dot(a, b, preferred_element_type=F32)


def _rms(xf, g):
    return xf * lax.rsqrt(jnp.mean(xf * xf, axis=-1, keepdims=True) + EPS) * g


def _params(n_axes, limit=VMEM_LIMIT):
    return pltpu.CompilerParams(dimension_semantics=("arbitrary",) * n_axes, vmem_limit_bytes=limit)


def _resident(shape):
    nd = len(shape)
    return pl.BlockSpec(shape, lambda *_: (0,) * nd, pipeline_mode=pl.Buffered(1))


def _proj_even_kernel(x_ref, g_ref, wt_ref, ca_ref, sa_ref, cr_ref, sr_ref, cc_ref, sc_ref, qn_ref, kn_ref,
                      qta_ref, ka_ref, vta_ref, qtb_ref, kb_ref, vtb_ref):
    h = _rms(x_ref[...], g_ref[...]).astype(BF16)

    def proj_t(lo, hi):
        return lax.dot_general(wt_ref[lo:hi, :], h, _NT, preferred_element_type=F32)

    ca, sa = ca_ref[...], sa_ref[...]
    cr, sr = cr_ref[...], sr_ref[...]
    cc, sc = cc_ref[...], sc_ref[...]

    def rope_a(blk):
        x1, x2 = blk[0:8], blk[8:16]
        return jnp.concatenate([x1 * ca - x2 * sa, x2 * ca + x1 * sa, blk[16:]], axis=0)

    def norm_rope_b(blk, gain):
        n = blk * lax.rsqrt(jnp.mean(blk * blk, axis=0, keepdims=True) + EPS) * gain
        r1, r2, c1, c2 = n[0:16], n[16:32], n[32:48], n[48:64]
        return jnp.concatenate([r1 * cr - r2 * sr, r2 * cr + r1 * sr,
                                c1 * cc - c2 * sc, c2 * cc + c1 * sc], axis=0)

    o = 0
    aq = proj_t(o, o + A_QK)
    qta_ref[0] = (jnp.concatenate([rope_a(aq[m * 64:(m + 1) * 64]) for m in range(2 * A_HEADS)], axis=0)
                  * Q_SCALE).astype(BF16)
    o += A_QK
    ak = proj_t(o, o + A_QK)
    for hh in range(A_HEADS):
        kt = jnp.concatenate([rope_a(ak[(2 * hh + m) * 64:(2 * hh + m + 1) * 64]) for m in range(2)], axis=0)
        ka_ref[0, hh] = kt.T.astype(BF16)
    o += A_QK
    vta_ref[0] = proj_t(o, o + A_V).astype(BF16)
    o += A_V
    bq = proj_t(o, o + B_Q)
    qn = qn_ref[...]
    qtb_ref[0] = (jnp.concatenate([norm_rope_b(bq[m * 64:(m + 1) * 64], qn) for m in range(B_Q_HEADS)], axis=0)
                  * Q_SCALE).astype(BF16)
    o += B_Q
    bk = proj_t(o, o + B_KV)
    kn = kn_ref[...]
    kbt = jnp.concatenate([norm_rope_b(bk[m * 64:(m + 1) * 64], kn) for m in range(B_KV_HEADS)], axis=0)
    kb_ref[0] = kbt.T.astype(BF16)
    o += B_KV
    bv = proj_t(o, o + B_KV)
    pad = jnp.concatenate([jnp.ones((8, TM), F32), jnp.zeros((VB_ROWS - HEAD_DIM - 8, TM), F32)], axis=0)
    for gg in range(B_KV_HEADS):
        vtb_ref[0, gg] = jnp.concatenate([bv[gg * 64:(gg + 1) * 64], pad], axis=0).astype(BF16)


def _proj_even(xf, g, wt, tabs, qn, kn, batch, seq):
    t = xf.shape[0]
    ns = seq // TM
    tok = lambda i: (i // ns, 0, i % ns)
    tab = lambda rows: pl.BlockSpec((rows, TM), lambda i: (0, i % ns))
    out_shape = (
        jax.ShapeDtypeStruct((batch, A_QK, seq), BF16),
        jax.ShapeDtypeStruct((batch, A_HEADS, seq, 128), BF16),
        jax.ShapeDtypeStruct((batch, A_V, seq), BF16),
        jax.ShapeDtypeStruct((batch, B_Q, seq), BF16),
        jax.ShapeDtypeStruct((batch, seq, B_KV), BF16),
        jax.ShapeDtypeStruct((batch, B_KV_HEADS, VB_ROWS, seq), BF16),
    )
    return pl.pallas_call(
        _proj_even_kernel,
        grid=(t // TM,),
        in_specs=[pl.BlockSpec((TM, D_MODEL), lambda i: (i, 0)),
                  _resident((1, D_MODEL)),
                  _resident((EVEN_IN, D_MODEL)),
                  tab(8), tab(8), tab(16), tab(16), tab(16), tab(16),
                  _resident((HEAD_DIM, 1)), _resident((HEAD_DIM, 1))],
        out_specs=(pl.BlockSpec((1, A_QK, TM), tok),
                   pl.BlockSpec((1, A_HEADS, TM, 128), lambda i: (i // ns, 0, i % ns, 0)),
                   pl.BlockSpec((1, A_V, TM), tok),
                   pl.BlockSpec((1, B_Q, TM), tok),
                   pl.BlockSpec((1, TM, B_KV), lambda i: (i // ns, i % ns, 0)),
                   pl.BlockSpec((1, B_KV_HEADS, VB_ROWS, TM), lambda i: (i // ns, 0, 0, i % ns))),
        out_shape=out_shape,
        compiler_params=_params(1),
        name="proj_even",
    )(xf, g, wt, *tabs, qn, kn)


def _flash_tiles(load_w, scores_fn, values_fn, finalize_fn, n_tiles, n_blocks, w_scr, s_scr, p_scr, a_scr, m_scr,
                 l_scr, acc_scr):
    n_slots = s_scr.shape[0]
    assert n_blocks % n_slots == 0

    def reset():
        m_scr[...] = jnp.full(m_scr.shape, -jnp.inf, F32)
        acc_scr[...] = jnp.zeros(acc_scr.shape, F32)
        if l_scr is not None:
            l_scr[...] = jnp.zeros(l_scr.shape, F32)

    def produce(j, slot, w):
        s_t = scores_fn(j, w)
        s_scr[slot] = s_t
        return jnp.max(s_t, axis=0, keepdims=True)

    def softmax(slot, blk_max):
        m_old = m_scr[...]
        m_new = jnp.maximum(m_old, blk_max)
        alpha = jnp.exp2(m_old - m_new)
        p = jnp.exp2(s_scr[slot] - m_new)
        if l_scr is not None:
            l_scr[...] = alpha * l_scr[...] + jnp.sum(p, axis=0, keepdims=True)
        p_scr[slot] = p.astype(BF16)
        a_scr[slot] = alpha
        m_scr[...] = m_new

    def accumulate(j, slot):
        acc_scr[...] = a_scr[slot] * acc_scr[...] + _dot(values_fn(j), p_scr[slot])

    def body(t, blk_max):
        cur = t % 2
        for u in range(n_blocks):
            if u + 1 < n_blocks:
                nxt = produce(u + 1, (u + 1) % n_slots, w_scr[cur])
            else:
                w_next = load_w(jnp.minimum(t + 1, n_tiles - 1))
                w_scr[1 - cur] = w_next
                nxt = produce(0, 0, w_next)
            softmax(u % n_slots, blk_max)
            if u >= 1:
                accumulate(u - 1, (u - 1) % n_slots)
            blk_max = nxt
        accumulate(n_blocks - 1, (n_blocks - 1) % n_slots)
        finalize_fn(t)
        reset()
        return blk_max

    reset()
    w_scr[0] = load_w(0)
    lax.fori_loop(0, n_tiles, body, produce(0, 0, w_scr[0]))


def _attn_a_kernel(q_ref, k_ref, v_ref, lq1_ref, lk1_ref, lq2_ref, lk2_ref, sub_ref, o_ref,
                   w_scr, s_scr, p_scr, a_scr, m_scr, l_scr, acc_scr, *, lam_init, seq):
    tile = lambda t: pl.ds(pl.multiple_of(t * TQ_A, TQ_A), TQ_A)
    blk = lambda j: pl.ds(j * TK_A, TK_A)

    def load_w(t):
        qt = q_ref[0, :, tile(t)].astype(F32)
        row = lax.broadcasted_iota(jnp.int32, qt.shape, 0)
        zero = jnp.zeros_like(qt)
        return jnp.concatenate([jnp.where(row < HEAD_DIM, qt, zero), jnp.where(row >= HEAD_DIM, qt, zero)],
                               axis=1).astype(BF16)

    def finalize(t):
        o = acc_scr[...] / l_scr[...]
        lam = (jnp.exp(jnp.sum(lq1_ref[...] * lk1_ref[...], axis=-1, keepdims=True))
               - jnp.exp(jnp.sum(lq2_ref[...] * lk2_ref[...], axis=-1, keepdims=True)) + lam_init)
        d = o[:, :TQ_A] - lam * o[:, TQ_A:]
        y = d * lax.rsqrt(jnp.mean(d * d, axis=0, keepdims=True) + EPS) * sub_ref[...]
        o_ref[0, tile(t), :] = (y * (1.0 - lam_init)).T.astype(BF16)

    _flash_tiles(load_w, lambda j, w: _dot(k_ref[0, 0, blk(j), :], w),
                 lambda j: v_ref[0, :, blk(j)], finalize, seq // TQ_A, seq // TK_A,
                 w_scr, s_scr, p_scr, a_scr, m_scr, l_scr, acc_scr)


def _attn_a(qta, ka, vta, lq1, lk1, lq2, lk2, subln, lam_init):
    batch, _, seq = qta.shape
    vec = _resident((1, HEAD_DIM))
    n = 2 * TQ_A
    return pl.pallas_call(
        functools.partial(_attn_a_kernel, lam_init=lam_init, seq=seq),
        grid=(batch, A_HEADS),
        in_specs=[pl.BlockSpec((1, 128, seq), lambda b, h: (b, h, 0)),
                  pl.BlockSpec((1, 1, seq, 128), lambda b, h: (b, h, 0, 0)),
                  pl.BlockSpec((1, 128, seq), lambda b, h: (b, h, 0)),
                  vec, vec, vec, vec,
                  _resident((128, 1))],
        out_specs=pl.BlockSpec((1, seq, 128), lambda b, h: (b, 0, h)),
        out_shape=jax.ShapeDtypeStruct((batch, seq, A_V), BF16),
        scratch_shapes=[pltpu.VMEM((2, 128, n), BF16), pltpu.VMEM((SCORE_SLOTS, TK_A, n), F32),
                        pltpu.VMEM((SCORE_SLOTS, TK_A, n), BF16), pltpu.VMEM((SCORE_SLOTS, 1, n), F32),
                        pltpu.VMEM((1, n), F32), pltpu.VMEM((1, n), F32), pltpu.VMEM((128, n), F32)],
        compiler_params=_params(2),
        name="attn_diff",
    )(qta, ka, vta, lq1, lk1, lq2, lk2, subln)


def _attn_b_kernel(q_ref, k_ref, v_ref, o_ref, w_scr, s_scr, p_scr, a_scr, m_scr, acc_scr, *, seq):
    g = pl.program_id(1)
    rep = B_Q_HEADS // B_KV_HEADS
    tile = lambda t: pl.ds(pl.multiple_of(t * TQ_B, TQ_B), TQ_B)
    blk = lambda j: pl.ds(j * TK_B, TK_B)

    def load_w(t):
        qt = q_ref[0, :, tile(t)].astype(F32)
        wq = jnp.concatenate([qt[r * 64:(r + 1) * 64] for r in range(rep)], axis=1)
        zero = jnp.zeros_like(wq)
        return jnp.concatenate([jnp.where(g == gg, wq, zero) for gg in range(B_KV_HEADS)], axis=0).astype(BF16)

    def finalize(t):
        o = acc_scr[0:HEAD_DIM, :] / acc_scr[HEAD_DIM:HEAD_DIM + 1, :]
        ot = jnp.concatenate([o[:, r * TQ_B:(r + 1) * TQ_B] for r in range(rep)], axis=0)
        o_ref[0, tile(t), :] = ot.T.astype(BF16)

    _flash_tiles(load_w, lambda j, w: _dot(k_ref[0, blk(j), :], w),
                 lambda j: v_ref[0, 0, :, blk(j)], finalize, seq // TQ_B, seq // TK_B,
                 w_scr, s_scr, p_scr, a_scr, m_scr, None, acc_scr)


def _attn_b(qtb, kb, vtb):
    batch, _, seq = qtb.shape
    rep = B_Q_HEADS // B_KV_HEADS
    n = rep * TQ_B
    return pl.pallas_call(
        functools.partial(_attn_b_kernel, seq=seq),
        grid=(batch, B_KV_HEADS),
        in_specs=[pl.BlockSpec((1, rep * HEAD_DIM, seq), lambda b, g: (b, g, 0)),
                  pl.BlockSpec((1, seq, B_KV), lambda b, g: (b, 0, 0)),
                  pl.BlockSpec((1, 1, VB_ROWS, seq), lambda b, g: (b, g, 0, 0))],
        out_specs=pl.BlockSpec((1, seq, rep * HEAD_DIM), lambda b, g: (b, 0, g)),
        out_shape=jax.ShapeDtypeStruct((batch, seq, B_Q), BF16),
        scratch_shapes=[pltpu.VMEM((2, B_KV, n), BF16), pltpu.VMEM((SCORE_SLOTS, TK_B, n), F32),
                        pltpu.VMEM((SCORE_SLOTS, TK_B, n), BF16), pltpu.VMEM((SCORE_SLOTS, 1, n), F32),
                        pltpu.VMEM((1, n), F32), pltpu.VMEM((VB_ROWS, n), F32)],
        compiler_params=_params(2),
        name="attn_gqa",
    )(qtb, kb, vtb)


def _proj_odd_kernel(x_ref, g_ref, wqt_ref, wk_ref, wvt_ref, qt_ref, k_ref, vt_ref):
    h = _rms(x_ref[...], g_ref[...]).astype(BF16)
    qt_ref[0] = (lax.dot_general(wqt_ref[...], h, _NT, preferred_element_type=F32) * Q_SCALE).astype(BF16)
    k_ref[0] = _dot(h, wk_ref[...]).astype(BF16)
    vt_ref[0] = lax.dot_general(wvt_ref[...], h, _NT, preferred_element_type=F32).astype(BF16)


def _proj_odd(xf, g, wqt, wk, wvt, batch, seq):
    t = xf.shape[0]
    ns = seq // TM
    tok_t = pl.BlockSpec((1, C_MIX, TM), lambda i: (i // ns, 0, i % ns))
    sq = _resident((C_MIX, D_MODEL))
    return pl.pallas_call(
        _proj_odd_kernel,
        grid=(t // TM,),
        in_specs=[pl.BlockSpec((TM, D_MODEL), lambda i: (i, 0)), _resident((1, D_MODEL)),
                  sq, _resident((D_MODEL, C_MIX)), sq],
        out_specs=(tok_t, pl.BlockSpec((1, TM, C_MIX), lambda i: (i // ns, i % ns, 0)), tok_t),
        out_shape=(jax.ShapeDtypeStruct((batch, C_MIX, seq), BF16),
                   jax.ShapeDtypeStruct((batch, seq, C_MIX), BF16),
                   jax.ShapeDtypeStruct((batch, C_MIX, seq), BF16)),
        compiler_params=_params(1),
        name="proj_odd",
    )(xf, g, wqt, wk, wvt)


def _na_bias_kernel(tab_ref, o_ref):
    h = pl.program_id(0)
    shp = (GRID_W, 2 * GRID_W)
    kc = lax.broadcasted_iota(jnp.int32, shp, 0)
    lane = lax.broadcasted_iota(jnp.int32, shp, 1)
    qc = lane & (GRID_W - 1)
    cs = jnp.clip(qc - NA_KW // 2, 0, GRID_W - NA_KW)
    col_valid = (kc >= cs) & (kc < cs + NA_KW)
    dci = jnp.clip(kc - qc + NA_KW - 1, 0, 2 * NA_KW - 2)
    neg = jnp.full(shp, -jnp.inf, F32)
    n_dc = 2 * NA_KW - 1
    n_dr = 2 * NA_KH - 1
    planes = []
    for dr in range(n_dr):
        acc = jnp.zeros(shp, F32)
        for dd in range(n_dc):
            acc = jnp.where(dci == dd, tab_ref[h, dr * n_dc + dd], acc)
        planes.append(jnp.where(col_valid, acc * LOG2E, neg))
    for i in range(NA_PT):
        dl, dr_ = i - NA_PT_OFF, i - NA_PT_OFF - 1
        left = planes[dl] if 0 <= dl < n_dr else neg
        right = planes[dr_] if 0 <= dr_ < n_dr else neg
        o_ref[0, i] = jnp.where(lane < GRID_W, left, right)


def _na_bias(rel_bias):
    tab = rel_bias.reshape(C_HEADS, -1)
    return pl.pallas_call(
        _na_bias_kernel,
        grid=(C_HEADS,),
        in_specs=[pl.BlockSpec(memory_space=pltpu.SMEM)],
        out_specs=pl.BlockSpec((1, NA_PT, GRID_W, 2 * GRID_W), lambda h: (h, 0, 0, 0)),
        out_shape=jax.ShapeDtypeStruct((C_HEADS, NA_PT, GRID_W, 2 * GRID_W), F32),
        compiler_params=_params(1),
        name="na_bias_table",
    )(tab)


def _na_window_start(r0, rows):
    return jnp.clip(r0 - NA_KH // 2, 0, rows - NA_WIN)


def _na_mask_kernel(o_ref, *, rows):
    c = pl.program_id(0)
    r0 = jnp.where(c == 0, 0, jnp.where(c == 1, NA_ROWS, rows - NA_ROWS))
    ws = _na_window_start(r0, rows)
    shp = (NA_WIN * GRID_W, NA_ROWS * GRID_W)
    kr = ws + lax.broadcasted_iota(jnp.int32, shp, 0) // GRID_W
    r = r0 + lax.broadcasted_iota(jnp.int32, shp, 1) // GRID_W
    rs = jnp.clip(r - NA_KH // 2, 0, rows - NA_KH)
    o_ref[0] = jnp.where((kr >= rs) & (kr < rs + NA_KH), 0.0, -jnp.inf).astype(F32)


def _na_mask(rows):
    shp = (NA_WIN * GRID_W, NA_ROWS * GRID_W)
    return pl.pallas_call(
        functools.partial(_na_mask_kernel, rows=rows),
        grid=(3,),
        out_specs=pl.BlockSpec((1,) + shp, lambda c: (c, 0, 0)),
        out_shape=jax.ShapeDtypeStruct((3,) + shp, F32),
        compiler_params=_params(1),
        name="na_row_mask",
    )()


def _na_kernel(q_ref, k_ref, v_ref, pt_ref, mask_ref, o_ref, s_scr, *, rows):
    nk = NA_WIN * GRID_W
    nq = NA_ROWS * GRID_W
    n_groups = rows // NA_ROWS
    hrow = lax.broadcasted_iota(jnp.int32, (2 * HEAD_DIM, nq), 0) // HEAD_DIM

    def window(g):
        ws = _na_window_start(g * NA_ROWS, rows)
        return ws, pl.ds(pl.multiple_of(ws * GRID_W, 2 * GRID_W), nk)

    def produce(g, slot):
        ws, win = window(g)
        qt = q_ref[0, :, pl.ds(pl.multiple_of(g * nq, nq), nq)].astype(F32)
        zero = jnp.zeros_like(qt)
        w = jnp.concatenate([jnp.where(hrow == hh, qt, zero) for hh in range(2)], axis=1).astype(BF16)
        s_t = _dot(k_ref[0, win, :], w)
        case = jnp.where(g == 0, 0, jnp.where(g == n_groups - 1, 2, 1))
        for j in range(NA_WIN):
            rsl = slice(j * GRID_W, (j + 1) * GRID_W)
            for hh in range(2):
                for a0 in range(0, NA_ROWS, 2):
                    idx = ws - g * NA_ROWS + (j - a0 + NA_KH - 1 + NA_PT_OFF)
                    msl = slice(a0 * GRID_W, (a0 + 2) * GRID_W)
                    csl = slice(hh * nq + a0 * GRID_W, hh * nq + (a0 + 2) * GRID_W)
                    s_scr[slot, rsl, csl] = s_t[rsl, csl] + pt_ref[hh, idx] + mask_ref[case, rsl, msl]
        return jnp.max(s_scr[slot], axis=0, keepdims=True)

    def consume(g, slot, blk_max):
        _, win = window(g)
        p = jnp.exp2(s_scr[slot] - blk_max)
        l = jnp.sum(p, axis=0, keepdims=True)
        ot = _dot(v_ref[0, :, win], p.astype(BF16)) / l
        o = jnp.concatenate([ot[hh * HEAD_DIM:(hh + 1) * HEAD_DIM, hh * nq:(hh + 1) * nq] for hh in range(2)],
                            axis=0)
        o_ref[0, pl.ds(pl.multiple_of(g * nq, nq), nq), :] = o.T.astype(BF16)

    per_trip = min(NA_GROUPS_PER_TRIP, n_groups)
    assert per_trip % 2 == 0 and n_groups % per_trip == 0

    def body(i, blk_max):
        g = per_trip * i
        for u in range(per_trip):
            nxt = produce(jnp.minimum(g + u + 1, n_groups - 1), (u + 1) % 2)
            consume(g + u, u % 2, blk_max)
            blk_max = nxt
        return blk_max

    lax.fori_loop(0, n_groups // per_trip, body, produce(0, 0))


def _na(qt, k, vt, pt, mask):
    batch, seq, _ = k.shape
    rows = seq // GRID_W
    assert rows % 2 == 0 and rows >= NA_WIN + NA_ROWS
    nq = NA_ROWS * GRID_W
    pairs = C_HEADS // 2
    tr = pl.BlockSpec((1, 2 * HEAD_DIM, seq), lambda b, h: (b, h, 0))
    nat = pl.BlockSpec((1, seq, 2 * HEAD_DIM), lambda b, h: (b, 0, h))
    return pl.pallas_call(
        functools.partial(_na_kernel, rows=rows),
        grid=(batch, pairs),
        in_specs=[tr, nat, tr,
                  pl.BlockSpec((2, NA_PT, GRID_W, 2 * GRID_W), lambda b, h: (h, 0, 0, 0)),
                  _resident((3, NA_WIN * GRID_W, nq))],
        out_specs=nat,
        out_shape=jax.ShapeDtypeStruct((batch, seq, C_MIX), BF16),
        scratch_shapes=[pltpu.VMEM((2, NA_WIN * GRID_W, 2 * nq), F32)],
        compiler_params=_params(2),
        name="attn_nbr",
    )(qt, k, vt, pt, mask)


def _outproj_kernel(*refs):
    n = (len(refs) - 2) // 2
    x_ref, o_ref = refs[0], refs[-1]
    y = x_ref[...]
    for a_ref, w_ref in zip(refs[1:1 + n], refs[1 + n:1 + 2 * n]):
        y = y + _dot(a_ref[...], w_ref[...])
    o_ref[...] = y


def _outproj(xf, mixes, weights):
    t = xf.shape[0]
    tile = lambda c: pl.BlockSpec((TM, c), lambda i: (i, 0))
    return pl.pallas_call(
        _outproj_kernel,
        grid=(t // TM,),
        in_specs=[tile(D_MODEL)] + [tile(m.shape[1]) for m in mixes] + [_resident(w.shape) for w in weights],
        out_specs=tile(D_MODEL),
        out_shape=jax.ShapeDtypeStruct((t, D_MODEL), F32),
        compiler_params=_params(1),
        name="out_proj",
    )(xf, *mixes, *weights)


def _ffn_kernel(xm_ref, xp_ref, xn_ref, g_ref, wu_ref, cw_ref, cb_ref, wd_ref, pg_ref, wg_ref, p_ref, wp_ref,
                fg_ref, o_ref, h_scr, acc_scr, u_scr, *, tiles_per_seq, final):
    i = pl.program_id(0)
    g = g_ref[...]
    xm = xm_ref[...]
    pos = i % tiles_per_seq
    hp = jnp.where(pos == 0, 0.0, _rms(xp_ref[...], g))
    hn = jnp.where(pos == tiles_per_seq - 1, 0.0, _rms(xn_ref[...], g))
    h_scr[...] = jnp.concatenate([hp, _rms(xm, g), hn], axis=0).astype(BF16)
    acc_scr[...] = jnp.zeros(acc_scr.shape, F32)

    def conv(u, col):
        cw = cw_ref[:, pl.ds(col, TF)]
        rows = u.shape[0]
        y = (pltpu.roll(u, 1, 0) * cw[0:1] + u * cw[1:2] + pltpu.roll(u, rows - 1, 0) * cw[2:3]
             + cb_ref[:, pl.ds(col, TF)])
        return y[HALO:HALO + TM_FFN]

    def up(c, slot):
        h = h_scr[...]
        u_scr[slot, 0] = _dot(h, wu_ref[:, pl.ds(c * TF, TF)])
        u_scr[slot, 1] = _dot(h, wu_ref[:, pl.ds(D_FF + c * TF, TF)])

    def act_down(c, slot):
        gate = conv(u_scr[slot, 0], c * TF)
        val = conv(u_scr[slot, 1], D_FF + c * TF)
        act = gate * jax.nn.sigmoid(gate) * val
        acc_scr[...] += _dot(act.astype(BF16), wd_ref[pl.ds(c * TF, TF), :])

    n_chunks = D_FF // TF
    n_slots = u_scr.shape[0]
    for c in range(min(n_slots - 1, n_chunks)):
        up(c, c)
    for c in range(n_chunks):
        ahead = c + n_slots - 1
        if ahead < n_chunks:
            up(ahead, ahead % n_slots)
        act_down(c, c % n_slots)

    x2 = xm + acc_scr[...]
    gate = jax.nn.sigmoid(_dot(_rms(x2, pg_ref[...]).astype(BF16), wg_ref[...]))
    x3 = x2 + gate * _dot(p_ref[...].astype(BF16), wp_ref[...])
    if final:
        x3 = _rms(x3, fg_ref[...])
    o_ref[...] = x3


def _ffn(xf, g, wu, cw, cb, wd, pg, wg, p, wp, fg, seq, final):
    t = xf.shape[0]
    tps = seq // TM_FFN
    nb = TM_FFN // HALO
    last = t // HALO - 1
    return pl.pallas_call(
        functools.partial(_ffn_kernel, tiles_per_seq=tps, final=final),
        grid=(t // TM_FFN,),
        in_specs=[pl.BlockSpec((TM_FFN, D_MODEL), lambda i: (i, 0)),
                  pl.BlockSpec((HALO, D_MODEL), lambda i: (jnp.maximum(i * nb - 1, 0), 0)),
                  pl.BlockSpec((HALO, D_MODEL), lambda i: (jnp.minimum((i + 1) * nb, last), 0)),
                  _resident((1, D_MODEL)),
                  _resident((D_MODEL, 2 * D_FF)),
                  _resident((3, 2 * D_FF)), _resident((1, 2 * D_FF)),
                  _resident((D_FF, D_MODEL)),
                  _resident((1, D_MODEL)),
                  _resident((D_MODEL, D_MODEL)),
                  pl.BlockSpec((TM_FFN, PLE_DIM), lambda i: (i, 0)),
                  _resident((PLE_DIM, D_MODEL)),
                  _resident((1, D_MODEL))],
        out_specs=pl.BlockSpec((TM_FFN, D_MODEL), lambda i: (i, 0)),
        out_shape=jax.ShapeDtypeStruct((t, D_MODEL), F32),
        scratch_shapes=[pltpu.VMEM((TM_FFN + 2 * HALO, D_MODEL), BF16), pltpu.VMEM((TM_FFN, D_MODEL), F32),
                        pltpu.VMEM((FFN_SLOTS, 2, TM_FFN + 2 * HALO, TF), F32)],
        compiler_params=_params(1),
        name="convffn_ple",
    )(xf, xf, xf, g, wu, cw, cb, wd, pg, wg, p, wp, fg)


def _rope_tables(seq):
    pos = jnp.arange(seq, dtype=jnp.int32)

    def tab(p, half, theta):
        inv = theta ** (-jnp.arange(half, dtype=F32) / half)
        ang = inv[:, None] * p.astype(F32)[None, :]
        return jnp.cos(ang), jnp.sin(ang)

    ca, sa = tab(pos, HEAD_DIM // 8, ROPE_THETA)
    cr, sr = tab(pos // GRID_W, HEAD_DIM // 4, AXIAL_THETA)
    cc, sc = tab(pos % GRID_W, HEAD_DIM // 4, AXIAL_THETA)
    return ca, sa, cr, sr, cc, sc


def kernel(x, p, attn_norm, w_in_ab, lambda_q1, lambda_k1, lambda_q2, lambda_k2, a_subln, b_q_norm, b_k_norm,
           w_out_ab, w_in_c, c_rel_bias, w_out_c, ffn_norm, w_ffn_up, ffn_conv_w, ffn_conv_b, w_ffn_down,
           ple_norm, w_ple_gate, w_ple_proj, final_norm):
    batch, seq, _ = x.shape
    depth = attn_norm.shape[0]
    t = batch * seq
    xf = x.reshape(t, D_MODEL)
    tabs = _rope_tables(seq)
    row = lambda v: v.reshape(1, -1)
    colv = lambda v: v.reshape(-1, 1)
    for i in range(depth):
        j = i // 2
        if i % 2 == 0:
            lam_init = 0.8 - 0.6 * math.exp(-0.3 * i)
            wt = w_in_ab[j].T.astype(BF16)
            qta, ka, vta, qtb, kb, vtb = _proj_even(xf, row(attn_norm[i]), wt, tabs, colv(b_q_norm[j]),
                                                    colv(b_k_norm[j]), batch, seq)
            mix_a = _attn_a(qta, ka, vta, row(lambda_q1[j]), row(lambda_k1[j]), row(lambda_q2[j]),
                            row(lambda_k2[j]), colv(a_subln[j]), lam_init)
            mix_b = _attn_b(qtb, kb, vtb)
            w_out = w_out_ab[j].astype(BF16)
            x1 = _outproj(xf, [mix_a.reshape(t, A_V), mix_b.reshape(t, B_Q)], [w_out[:A_V], w_out[A_V:]])
        else:
            w_c = w_in_c[j].astype(BF16)
            qt, k, vt = _proj_odd(xf, row(attn_norm[i]), w_c[:, :C_MIX].T, w_c[:, C_MIX:2 * C_MIX],
                                  w_c[:, 2 * C_MIX:].T, batch, seq)
            o = _na(qt, k, vt, _na_bias(c_rel_bias[j]), _na_mask(seq // GRID_W))
            x1 = _outproj(xf, [o.reshape(t, C_MIX)], [w_out_c[j].astype(BF16)])
        xf = _ffn(x1, row(ffn_norm[i]), w_ffn_up[i].astype(BF16), ffn_conv_w[i], row(ffn_conv_b[i]),
                  w_ffn_down[i].astype(BF16), row(ple_norm[i]), w_ple_gate[i].astype(BF16),
                  p[i].reshape(t, PLE_DIM), w_ple_proj[i].astype(BF16), row(final_norm), seq,
                  final=(i == depth - 1))
    return xf.reshape(batch, seq, D_MODEL)
```

```python
import functools
import math

import jax
import jax.numpy as jnp
from jax import lax
from jax.experimental import pallas as pl
from jax.experimental.pallas import tpu as pltpu

F32 = jnp.float32
BF16 = jnp.bfloat16

D_MODEL = 1024
HEAD_DIM = 64
A_HEADS = 4
B_Q_HEADS = 8
B_KV_HEADS = 2
C_HEADS = 16
ROPE_THETA = 500000.0
AXIAL_THETA = 10000.0
GRID_W = 64
NA_KH = 8
NA_KW = 16
D_FF = 2816
PLE_DIM = 256
EPS = 1e-6
SCALE = HEAD_DIM ** -0.5
LOG2E = math.log2(math.e)
Q_SCALE = SCALE * LOG2E

A_QK = A_HEADS * 2 * HEAD_DIM
A_V = A_HEADS * 2 * HEAD_DIM
B_Q = B_Q_HEADS * HEAD_DIM
B_KV = B_KV_HEADS * HEAD_DIM
EVEN_IN = 2 * A_QK + A_V + B_Q + 2 * B_KV
C_MIX = C_HEADS * HEAD_DIM

TM = 1024
TQ_A = 512
TQ_B = 256
TK_A = 512
TK_B = 256
VB_ROWS = HEAD_DIM + 16
SCORE_SLOTS = 2
TM_FFN = 256
TF = 256
FFN_SLOTS = 4
HALO = 8
NA_ROWS = 4
NA_WIN = NA_ROWS + NA_KH
VC_ROWS = 2 * HEAD_DIM + 16
NA_GROUPS_PER_TRIP = 16
VMEM_LIMIT = 52 * 1024 * 1024

_NT = (((1,), (1,)), ((), ()))


def _dot(a, b):
    return jnp.dot(a, b, preferred_element_type=F32)


def _rms(xf, g):
    return xf * lax.rsqrt(jnp.mean(xf * xf, axis=-1, keepdims=True) + EPS) * g


def _params(n_axes, limit=VMEM_LIMIT):
    return pltpu.CompilerParams(dimension_semantics=("arbitrary",) * n_axes, vmem_limit_bytes=limit)


def _resident(shape):
    nd = len(shape)
    return pl.BlockSpec(shape, lambda *_: (0,) * nd, pipeline_mode=pl.Buffered(1))


def _proj_even_kernel(x_ref, g_ref, wt_ref, ca_ref, sa_ref, cr_ref, sr_ref, cc_ref, sc_ref, qn_ref, kn_ref,
                      qta_ref, ka_ref, vta_ref, qtb_ref, kb_ref, vtb_ref):
    h = _rms(x_ref[...], g_ref[...]).astype(BF16)

    def proj_t(lo, hi):
        return lax.dot_general(wt_ref[lo:hi, :], h, _NT, preferred_element_type=F32)

    ca, sa = ca_ref[...], sa_ref[...]
    cr, sr = cr_ref[...], sr_ref[...]
    cc, sc = cc_ref[...], sc_ref[...]

    def rope_a(blk):
        x1, x2 = blk[0:8], blk[8:16]
        return jnp.concatenate([x1 * ca - x2 * sa, x2 * ca + x1 * sa, blk[16:]], axis=0)

    def norm_rope_b(blk, gain):
        n = blk * lax.rsqrt(jnp.mean(blk * blk, axis=0, keepdims=True) + EPS) * gain
        r1, r2, c1, c2 = n[0:16], n[16:32], n[32:48], n[48:64]
        return jnp.concatenate([r1 * cr - r2 * sr, r2 * cr + r1 * sr,
                                c1 * cc - c2 * sc, c2 * cc + c1 * sc], axis=0)

    o = 0
    aq = proj_t(o, o + A_QK)
    qta_ref[0] = (jnp.concatenate([rope_a(aq[m * 64:(m + 1) * 64]) for m in range(2 * A_HEADS)], axis=0)
                  * Q_SCALE).astype(BF16)
    o += A_QK
    ak = proj_t(o, o + A_QK)
    for hh in range(A_HEADS):
        kt = jnp.concatenate([rope_a(ak[(2 * hh + m) * 64:(2 * hh + m + 1) * 64]) for m in range(2)], axis=0)
        ka_ref[0, hh] = kt.T.astype(BF16)
    o += A_QK
    vta_ref[0] = proj_t(o, o + A_V).astype(BF16)
    o += A_V
    bq = proj_t(o, o + B_Q)
    qn = qn_ref[...]
    qtb_ref[0] = (jnp.concatenate([norm_rope_b(bq[m * 64:(m + 1) * 64], qn) for m in range(B_Q_HEADS)], axis=0)
                  * Q_SCALE).astype(BF16)
    o += B_Q
    bk = proj_t(o, o + B_KV)
    kn = kn_ref[...]
    kbt = jnp.concatenate([norm_rope_b(bk[m * 64:(m + 1) * 64], kn) for m in range(B_KV_HEADS)], axis=0)
    kb_ref[0] = kbt.T.astype(BF16)
    o += B_KV
    bv = proj_t(o, o + B_KV)
    pad = jnp.concatenate([jnp.ones((8, TM), F32), jnp.zeros((VB_ROWS - HEAD_DIM - 8, TM), F32)], axis=0)
    for gg in range(B_KV_HEADS):
        vtb_ref[0, gg] = jnp.concatenate([bv[gg * 64:(gg + 1) * 64], pad], axis=0).astype(BF16)


def _proj_even(xf, g, wt, tabs, qn, kn, batch, seq):
    t = xf.shape[0]
    ns = seq // TM
    tok = lambda i: (i // ns, 0, i % ns)
    tab = lambda rows: pl.BlockSpec((rows, TM), lambda i: (0, i % ns))
    out_shape = (
        jax.ShapeDtypeStruct((batch, A_QK, seq), BF16),
        jax.ShapeDtypeStruct((batch, A_HEADS, seq, 128), BF16),
        jax.ShapeDtypeStruct((batch, A_V, seq), BF16),
        jax.ShapeDtypeStruct((batch, B_Q, seq), BF16),
        jax.ShapeDtypeStruct((batch, seq, B_KV), BF16),
        jax.ShapeDtypeStruct((batch, B_KV_HEADS, VB_ROWS, seq), BF16),
    )
    return pl.pallas_call(
        _proj_even_kernel,
        grid=(t // TM,),
        in_specs=[pl.BlockSpec((TM, D_MODEL), lambda i: (i, 0)),
                  _resident((1, D_MODEL)),
                  _resident((EVEN_IN, D_MODEL)),
                  tab(8), tab(8), tab(16), tab(16), tab(16), tab(16),
                  _resident((HEAD_DIM, 1)), _resident((HEAD_DIM, 1))],
        out_specs=(pl.BlockSpec((1, A_QK, TM), tok),
                   pl.BlockSpec((1, A_HEADS, TM, 128), lambda i: (i // ns, 0, i % ns, 0)),
                   pl.BlockSpec((1, A_V, TM), tok),
                   pl.BlockSpec((1, B_Q, TM), tok),
                   pl.BlockSpec((1, TM, B_KV), lambda i: (i // ns, i % ns, 0)),
                   pl.BlockSpec((1, B_KV_HEADS, VB_ROWS, TM), lambda i: (i // ns, 0, 0, i % ns))),
        out_shape=out_shape,
        compiler_params=_params(1),
        name="proj_even",
    )(xf, g, wt, *tabs, qn, kn)


def _flash_tiles(load_w, scores_fn, values_fn, finalize_fn, n_tiles, n_blocks, w_scr, s_scr, p_scr, a_scr, m_scr,
                 l_scr, acc_scr):
    n_slots = s_scr.shape[0]
    assert n_blocks % n_slots == 0

    def reset():
        m_scr[...] = jnp.full(m_scr.shape, -jnp.inf, F32)
        acc_scr[...] = jnp.zeros(acc_scr.shape, F32)
        if l_scr is not None:
            l_scr[...] = jnp.zeros(l_scr.shape, F32)

    def produce(j, slot, w):
        s_t = scores_fn(j, w)
        s_scr[slot] = s_t
        return jnp.max(s_t, axis=0, keepdims=True)

    def softmax(slot, blk_max):
        m_old = m_scr[...]
        m_new = jnp.maximum(m_old, blk_max)
        alpha = jnp.exp2(m_old - m_new)
        p = jnp.exp2(s_scr[slot] - m_new)
        if l_scr is not None:
            l_scr[...] = alpha * l_scr[...] + jnp.sum(p, axis=0, keepdims=True)
        p_scr[slot] = p.astype(BF16)
        a_scr[slot] = alpha
        m_scr[...] = m_new

    def accumulate(j, slot):
        acc_scr[...] = a_scr[slot] * acc_scr[...] + _dot(values_fn(j), p_scr[slot])

    def body(t, blk_max):
        cur = t % 2
        for u in range(n_blocks):
            if u + 1 < n_blocks:
                nxt = produce(u + 1, (u + 1) % n_slots, w_scr[cur])
            else:
                w_next = load_w(jnp.minimum(t + 1, n_tiles - 1))
                w_scr[1 - cur] = w_next
                nxt = produce(0, 0, w_next)
            softmax(u % n_slots, blk_max)
            if u >= 1:
                accumulate(u - 1, (u - 1) % n_slots)
            blk_max = nxt
        accumulate(n_blocks - 1, (n_blocks - 1) % n_slots)
        finalize_fn(t)
        reset()
        return blk_max

    reset()
    w_scr[0] = load_w(0)
    lax.fori_loop(0, n_tiles, body, produce(0, 0, w_scr[0]))


def _attn_a_kernel(q_ref, k_ref, v_ref, lq1_ref, lk1_ref, lq2_ref, lk2_ref, sub_ref, o_ref,
                   w_scr, s_scr, p_scr, a_scr, m_scr, l_scr, acc_scr, *, lam_init, seq):
    tile = lambda t: pl.ds(pl.multiple_of(t * TQ_A, TQ_A), TQ_A)
    blk = lambda j: pl.ds(j * TK_A, TK_A)

    def load_w(t):
        qt = q_ref[0, :, tile(t)].astype(F32)
        row = lax.broadcasted_iota(jnp.int32, qt.shape, 0)
        zero = jnp.zeros_like(qt)
        return jnp.concatenate([jnp.where(row < HEAD_DIM, qt, zero), jnp.where(row >= HEAD_DIM, qt, zero)],
                               axis=1).astype(BF16)

    def finalize(t):
        o = acc_scr[...] / l_scr[...]
        lam = (jnp.exp(jnp.sum(lq1_ref[...] * lk1_ref[...], axis=-1, keepdims=True))
               - jnp.exp(jnp.sum(lq2_ref[...] * lk2_ref[...], axis=-1, keepdims=True)) + lam_init)
        d = o[:, :TQ_A] - lam * o[:, TQ_A:]
        y = d * lax.rsqrt(jnp.mean(d * d, axis=0, keepdims=True) + EPS) * sub_ref[...]
        o_ref[0, tile(t), :] = (y * (1.0 - lam_init)).T.astype(BF16)

    _flash_tiles(load_w, lambda j, w: _dot(k_ref[0, 0, blk(j), :], w),
                 lambda j: v_ref[0, :, blk(j)], finalize, seq // TQ_A, seq // TK_A,
                 w_scr, s_scr, p_scr, a_scr, m_scr, l_scr, acc_scr)


def _attn_a(qta, ka, vta, lq1, lk1, lq2, lk2, subln, lam_init):
    batch, _, seq = qta.shape
    vec = _resident((1, HEAD_DIM))
    n = 2 * TQ_A
    return pl.pallas_call(
        functools.partial(_attn_a_kernel, lam_init=lam_init, seq=seq),
        grid=(batch, A_HEADS),
        in_specs=[pl.BlockSpec((1, 128, seq), lambda b, h: (b, h, 0)),
                  pl.BlockSpec((1, 1, seq, 128), lambda b, h: (b, h, 0, 0)),
                  pl.BlockSpec((1, 128, seq), lambda b, h: (b, h, 0)),
                  vec, vec, vec, vec,
                  _resident((128, 1))],
        out_specs=pl.BlockSpec((1, seq, 128), lambda b, h: (b, 0, h)),
        out_shape=jax.ShapeDtypeStruct((batch, seq, A_V), BF16),
        scratch_shapes=[pltpu.VMEM((2, 128, n), BF16), pltpu.VMEM((SCORE_SLOTS, TK_A, n), F32),
                        pltpu.VMEM((SCORE_SLOTS, TK_A, n), BF16), pltpu.VMEM((SCORE_SLOTS, 1, n), F32),
                        pltpu.VMEM((1, n), F32), pltpu.VMEM((1, n), F32), pltpu.VMEM((128, n), F32)],
        compiler_params=_params(2),
        name="attn_diff",
    )(qta, ka, vta, lq1, lk1, lq2, lk2, subln)


def _attn_b_kernel(q_ref, k_ref, v_ref, o_ref, w_scr, s_scr, p_scr, a_scr, m_scr, acc_scr, *, seq):
    g = pl.program_id(1)
    rep = B_Q_HEADS // B_KV_HEADS
    tile = lambda t: pl.ds(pl.multiple_of(t * TQ_B, TQ_B), TQ_B)
    blk = lambda j: pl.ds(j * TK_B, TK_B)

    def load_w(t):
        qt = q_ref[0, :, tile(t)].astype(F32)
        wq = jnp.concatenate([qt[r * 64:(r + 1) * 64] for r in range(rep)], axis=1)
        zero = jnp.zeros_like(wq)
        return jnp.concatenate([jnp.where(g == gg, wq, zero) for gg in range(B_KV_HEADS)], axis=0).astype(BF16)

    def finalize(t):
        o = acc_scr[0:HEAD_DIM, :] / acc_scr[HEAD_DIM:HEAD_DIM + 1, :]
        ot = jnp.concatenate([o[:, r * TQ_B:(r + 1) * TQ_B] for r in range(rep)], axis=0)
        o_ref[0, tile(t), :] = ot.T.astype(BF16)

    _flash_tiles(load_w, lambda j, w: _dot(k_ref[0, blk(j), :], w),
                 lambda j: v_ref[0, 0, :, blk(j)], finalize, seq // TQ_B, seq // TK_B,
                 w_scr, s_scr, p_scr, a_scr, m_scr, None, acc_scr)


def _attn_b(qtb, kb, vtb):
    batch, _, seq = qtb.shape
    rep = B_Q_HEADS // B_KV_HEADS
    n = rep * TQ_B
    return pl.pallas_call(
        functools.partial(_attn_b_kernel, seq=seq),
        grid=(batch, B_KV_HEADS),
        in_specs=[pl.BlockSpec((1, rep * HEAD_DIM, seq), lambda b, g: (b, g, 0)),
                  pl.BlockSpec((1, seq, B_KV), lambda b, g: (b, 0, 0)),
                  pl.BlockSpec((1, 1, VB_ROWS, seq), lambda b, g: (b, g, 0, 0))],
        out_specs=pl.BlockSpec((1, seq, rep * HEAD_DIM), lambda b, g: (b, 0, g)),
        out_shape=jax.ShapeDtypeStruct((batch, seq, B_Q), BF16),
        scratch_shapes=[pltpu.VMEM((2, B_KV, n), BF16), pltpu.VMEM((SCORE_SLOTS, TK_B, n), F32),
                        pltpu.VMEM((SCORE_SLOTS, TK_B, n), BF16), pltpu.VMEM((SCORE_SLOTS, 1, n), F32),
                        pltpu.VMEM((1, n), F32), pltpu.VMEM((VB_ROWS, n), F32)],
        compiler_params=_params(2),
        name="attn_gqa",
    )(qtb, kb, vtb)


def _proj_odd_kernel(x_ref, g_ref, wqt_ref, wk_ref, wvt_ref, qt_ref, k_ref, vt_ref):
    h = _rms(x_ref[...], g_ref[...]).astype(BF16)
    qt_ref[0] = (lax.dot_general(wqt_ref[...], h, _NT, preferred_element_type=F32) * Q_SCALE).astype(BF16)
    k_ref[0] = _dot(h, wk_ref[...]).astype(BF16)
    vt = lax.dot_general(wvt_ref[...], h, _NT, preferred_element_type=F32)
    pad = jnp.concatenate([jnp.ones((8, TM), F32), jnp.zeros((VC_ROWS - 2 * HEAD_DIM - 8, TM), F32)], axis=0)
    for pr in range(C_HEADS // 2):
        vt_ref[0, pr] = jnp.concatenate([vt[pr * 2 * HEAD_DIM:(pr + 1) * 2 * HEAD_DIM], pad], axis=0).astype(BF16)


def _proj_odd(xf, g, wqt, wk, wvt, batch, seq):
    t = xf.shape[0]
    ns = seq // TM
    tok_t = pl.BlockSpec((1, C_MIX, TM), lambda i: (i // ns, 0, i % ns))
    sq = _resident((C_MIX, D_MODEL))
    return pl.pallas_call(
        _proj_odd_kernel,
        grid=(t // TM,),
        in_specs=[pl.BlockSpec((TM, D_MODEL), lambda i: (i, 0)), _resident((1, D_MODEL)),
                  sq, _resident((D_MODEL, C_MIX)), sq],
        out_specs=(tok_t, pl.BlockSpec((1, TM, C_MIX), lambda i: (i // ns, i % ns, 0)),
                   pl.BlockSpec((1, C_HEADS // 2, VC_ROWS, TM), lambda i: (i // ns, 0, 0, i % ns))),
        out_shape=(jax.ShapeDtypeStruct((batch, C_MIX, seq), BF16),
                   jax.ShapeDtypeStruct((batch, seq, C_MIX), BF16),
                   jax.ShapeDtypeStruct((batch, C_HEADS // 2, VC_ROWS, seq), BF16)),
        compiler_params=_params(1),
        name="proj_odd",
    )(xf, g, wqt, wk, wvt)


def _na_window_start(r0, rows):
    return jnp.clip(r0 - NA_KH // 2, 0, rows - NA_WIN)


def _na_bias_kernel(tab_ref, o_ref, *, rows):
    h = pl.program_id(0)
    shp = (GRID_W, 2 * GRID_W)
    kc = lax.broadcasted_iota(jnp.int32, shp, 0)
    lane = lax.broadcasted_iota(jnp.int32, shp, 1)
    qc = lane & (GRID_W - 1)
    cs = jnp.clip(qc - NA_KW // 2, 0, GRID_W - NA_KW)
    col_valid = (kc >= cs) & (kc < cs + NA_KW)
    dci = jnp.clip(kc - qc + NA_KW - 1, 0, 2 * NA_KW - 2)
    neg = jnp.full(shp, -jnp.inf, F32)
    n_dc = 2 * NA_KW - 1
    n_dr = 2 * NA_KH - 1
    planes = []
    for dr in range(n_dr):
        acc = jnp.zeros(shp, F32)
        for dd in range(n_dc):
            acc = jnp.where(dci == dd, tab_ref[h, dr * n_dc + dd], acc)
        planes.append(jnp.where(col_valid, acc * LOG2E, neg))
    clamp = lambda v, lo, hi: max(lo, min(v, hi))
    for c, r0 in enumerate((0, NA_ROWS, rows - NA_ROWS)):
        ws = clamp(r0 - NA_KH // 2, 0, rows - NA_WIN)
        for j in range(NA_WIN):
            kr = ws + j
            for a0 in range(0, NA_ROWS, 2):
                halves = []
                for r in (r0 + a0, r0 + a0 + 1):
                    rs = clamp(r - NA_KH // 2, 0, rows - NA_KH)
                    halves.append(planes[kr - r + NA_KH - 1] if rs <= kr < rs + NA_KH else neg)
                o_ref[0, c, j * GRID_W:(j + 1) * GRID_W, a0 * GRID_W:(a0 + 2) * GRID_W] = (
                    jnp.where(lane < GRID_W, halves[0], halves[1]))


def _na_bias(rel_bias, rows):
    tab = rel_bias.reshape(C_HEADS, -1)
    shp = (3, NA_WIN * GRID_W, NA_ROWS * GRID_W)
    return pl.pallas_call(
        functools.partial(_na_bias_kernel, rows=rows),
        grid=(C_HEADS,),
        in_specs=[pl.BlockSpec(memory_space=pltpu.SMEM)],
        out_specs=pl.BlockSpec((1,) + shp, lambda h: (h, 0, 0, 0)),
        out_shape=jax.ShapeDtypeStruct((C_HEADS,) + shp, F32),
        compiler_params=_params(1),
        name="na_bias_tiles",
    )(tab)


def _na_kernel(q_ref, k_ref, v_ref, bias_ref, o_ref, s_scr, *, rows):
    nk = NA_WIN * GRID_W
    nq = NA_ROWS * GRID_W
    n_groups = rows // NA_ROWS
    hrow = lax.broadcasted_iota(jnp.int32, (2 * HEAD_DIM, nq), 0) // HEAD_DIM

    def window(g):
        ws = _na_window_start(g * NA_ROWS, rows)
        return pl.ds(pl.multiple_of(ws * GRID_W, 2 * GRID_W), nk)

    def produce(g, slot):
        qt = q_ref[0, :, pl.ds(pl.multiple_of(g * nq, nq), nq)].astype(F32)
        zero = jnp.zeros_like(qt)
        w = jnp.concatenate([jnp.where(hrow == hh, qt, zero) for hh in range(2)], axis=1).astype(BF16)
        case = jnp.where(g == 0, 0, jnp.where(g == n_groups - 1, 2, 1))
        bias = jnp.concatenate([bias_ref[hh, case] for hh in range(2)], axis=1)
        s_t = _dot(k_ref[0, window(g), :], w) + bias
        s_scr[slot] = s_t
        return jnp.max(s_t, axis=0, keepdims=True)

    def consume(g, slot, blk_max):
        p = jnp.exp2(s_scr[slot] - blk_max)
        ot = _dot(v_ref[0, 0, :, window(g)], p.astype(BF16))
        ot = ot[0:2 * HEAD_DIM] / ot[2 * HEAD_DIM:2 * HEAD_DIM + 1]
        o = jnp.concatenate([ot[hh * HEAD_DIM:(hh + 1) * HEAD_DIM, hh * nq:(hh + 1) * nq] for hh in range(2)],
                            axis=0)
        o_ref[0, pl.ds(pl.multiple_of(g * nq, nq), nq), :] = o.T.astype(BF16)

    per_trip = min(NA_GROUPS_PER_TRIP, n_groups)
    assert per_trip % 2 == 0 and n_groups % per_trip == 0

    def body(i, blk_max):
        g = per_trip * i
        for u in range(per_trip):
            nxt = produce(jnp.minimum(g + u + 1, n_groups - 1), (u + 1) % 2)
            consume(g + u, u % 2, blk_max)
            blk_max = nxt
        return blk_max

    lax.fori_loop(0, n_groups // per_trip, body, produce(0, 0))


def _na(qt, k, vt, bias):
    batch, seq, _ = k.shape
    rows = seq // GRID_W
    assert rows % 2 == 0 and rows >= NA_WIN + NA_ROWS
    nq = NA_ROWS * GRID_W
    pairs = C_HEADS // 2
    nat = pl.BlockSpec((1, seq, 2 * HEAD_DIM), lambda b, h: (b, 0, h))
    return pl.pallas_call(
        functools.partial(_na_kernel, rows=rows),
        grid=(batch, pairs),
        in_specs=[pl.BlockSpec((1, 2 * HEAD_DIM, seq), lambda b, h: (b, h, 0)),
                  nat,
                  pl.BlockSpec((1, 1, VC_ROWS, seq), lambda b, h: (b, h, 0, 0)),
                  pl.BlockSpec((2, 3, NA_WIN * GRID_W, nq), lambda b, h: (h, 0, 0, 0))],
        out_specs=nat,
        out_shape=jax.ShapeDtypeStruct((batch, seq, C_MIX), BF16),
        scratch_shapes=[pltpu.VMEM((2, NA_WIN * GRID_W, 2 * nq), F32)],
        compiler_params=_params(2),
        name="attn_nbr",
    )(qt, k, vt, bias)


def _outproj_kernel(*refs):
    n = (len(refs) - 2) // 2
    x_ref, o_ref = refs[0], refs[-1]
    y = x_ref[...]
    for a_ref, w_ref in zip(refs[1:1 + n], refs[1 + n:1 + 2 * n]):
        y = y + _dot(a_ref[...], w_ref[...])
    o_ref[...] = y


def _outproj(xf, mixes, weights):
    t = xf.shape[0]
    tile = lambda c: pl.BlockSpec((TM, c), lambda i: (i, 0))
    return pl.pallas_call(
        _outproj_kernel,
        grid=(t // TM,),
        in_specs=[tile(D_MODEL)] + [tile(m.shape[1]) for m in mixes] + [_resident(w.shape) for w in weights],
        out_specs=tile(D_MODEL),
        out_shape=jax.ShapeDtypeStruct((t, D_MODEL), F32),
        compiler_params=_params(1),
        name="out_proj",
    )(xf, *mixes, *weights)


def _ffn_kernel(xm_ref, xp_ref, xn_ref, g_ref, wu_ref, cw_ref, cb_ref, wd_ref, pg_ref, wg_ref, p_ref, wp_ref,
                fg_ref, o_ref, h_scr, acc_scr, u_scr, *, tiles_per_seq, final):
    i = pl.program_id(0)
    g = g_ref[...]
    xm = xm_ref[...]
    pos = i % tiles_per_seq
    hp = jnp.where(pos == 0, 0.0, _rms(xp_ref[...], g))
    hn = jnp.where(pos == tiles_per_seq - 1, 0.0, _rms(xn_ref[...], g))
    h_scr[...] = jnp.concatenate([hp, _rms(xm, g), hn], axis=0).astype(BF16)
    acc_scr[...] = jnp.zeros(acc_scr.shape, F32)

    def conv(u, col):
        cw = cw_ref[:, pl.ds(col, TF)]
        rows = u.shape[0]
        y = (pltpu.roll(u, 1, 0) * cw[0:1] + u * cw[1:2] + pltpu.roll(u, rows - 1, 0) * cw[2:3]
             + cb_ref[:, pl.ds(col, TF)])
        return y[HALO:HALO + TM_FFN]

    def up(c, slot):
        h = h_scr[...]
        u_scr[slot, 0] = _dot(h, wu_ref[:, pl.ds(c * TF, TF)])
        u_scr[slot, 1] = _dot(h, wu_ref[:, pl.ds(D_FF + c * TF, TF)])

    def act_down(c, slot):
        gate = conv(u_scr[slot, 0], c * TF)
        val = conv(u_scr[slot, 1], D_FF + c * TF)
        act = gate * jax.nn.sigmoid(gate) * val
        acc_scr[...] += _dot(act.astype(BF16), wd_ref[pl.ds(c * TF, TF), :])

    n_chunks = D_FF // TF
    n_slots = u_scr.shape[0]
    for c in range(min(n_slots - 1, n_chunks)):
        up(c, c)
    for c in range(n_chunks):
        ahead = c + n_slots - 1
        if ahead < n_chunks:
            up(ahead, ahead % n_slots)
        act_down(c, c % n_slots)

    x2 = xm + acc_scr[...]
    gate = jax.nn.sigmoid(_dot(_rms(x2, pg_ref[...]).astype(BF16), wg_ref[...]))
    x3 = x2 + gate * _dot(p_ref[...].astype(BF16), wp_ref[...])
    if final:
        x3 = _rms(x3, fg_ref[...])
    o_ref[...] = x3


def _ffn(xf, g, wu, cw, cb, wd, pg, wg, p, wp, fg, seq, final):
    t = xf.shape[0]
    tps = seq // TM_FFN
    nb = TM_FFN // HALO
    last = t // HALO - 1
    return pl.pallas_call(
        functools.partial(_ffn_kernel, tiles_per_seq=tps, final=final),
        grid=(t // TM_FFN,),
        in_specs=[pl.BlockSpec((TM_FFN, D_MODEL), lambda i: (i, 0)),
                  pl.BlockSpec((HALO, D_MODEL), lambda i: (jnp.maximum(i * nb - 1, 0), 0)),
                  pl.BlockSpec((HALO, D_MODEL), lambda i: (jnp.minimum((i + 1) * nb, last), 0)),
                  _resident((1, D_MODEL)),
                  _resident((D_MODEL, 2 * D_FF)),
                  _resident((3, 2 * D_FF)), _resident((1, 2 * D_FF)),
                  _resident((D_FF, D_MODEL)),
                  _resident((1, D_MODEL)),
                  _resident((D_MODEL, D_MODEL)),
                  pl.BlockSpec((TM_FFN, PLE_DIM), lambda i: (i, 0)),
                  _resident((PLE_DIM, D_MODEL)),
                  _resident((1, D_MODEL))],
        out_specs=pl.BlockSpec((TM_FFN, D_MODEL), lambda i: (i, 0)),
        out_shape=jax.ShapeDtypeStruct((t, D_MODEL), F32),
        scratch_shapes=[pltpu.VMEM((TM_FFN + 2 * HALO, D_MODEL), BF16), pltpu.VMEM((TM_FFN, D_MODEL), F32),
                        pltpu.VMEM((FFN_SLOTS, 2, TM_FFN + 2 * HALO, TF), F32)],
        compiler_params=_params(1),
        name="convffn_ple",
    )(xf, xf, xf, g, wu, cw, cb, wd, pg, wg, p, wp, fg)


def _rope_tables(seq):
    pos = jnp.arange(seq, dtype=jnp.int32)

    def tab(p, half, theta):
        inv = theta ** (-jnp.arange(half, dtype=F32) / half)
        ang = inv[:, None] * p.astype(F32)[None, :]
        return jnp.cos(ang), jnp.sin(ang)

    ca, sa = tab(pos, HEAD_DIM // 8, ROPE_THETA)
    cr, sr = tab(pos // GRID_W, HEAD_DIM // 4, AXIAL_THETA)
    cc, sc = tab(pos % GRID_W, HEAD_DIM // 4, AXIAL_THETA)
    return ca, sa, cr, sr, cc, sc


def kernel(x, p, attn_norm, w_in_ab, lambda_q1, lambda_k1, lambda_q2, lambda_k2, a_subln, b_q_norm, b_k_norm,
           w_out_ab, w_in_c, c_rel_bias, w_out_c, ffn_norm, w_ffn_up, ffn_conv_w, ffn_conv_b, w_ffn_down,
           ple_norm, w_ple_gate, w_ple_proj, final_norm):
    batch, seq, _ = x.shape
    depth = attn_norm.shape[0]
    t = batch * seq
    xf = x.reshape(t, D_MODEL)
    tabs = _rope_tables(seq)
    row = lambda v: v.reshape(1, -1)
    colv = lambda v: v.reshape(-1, 1)
    for i in range(depth):
        j = i // 2
        if i % 2 == 0:
            lam_init = 0.8 - 0.6 * math.exp(-0.3 * i)
            wt = w_in_ab[j].T.astype(BF16)
            qta, ka, vta, qtb, kb, vtb = _proj_even(xf, row(attn_norm[i]), wt, tabs, colv(b_q_norm[j]),
                                                    colv(b_k_norm[j]), batch, seq)
            mix_a = _attn_a(qta, ka, vta, row(lambda_q1[j]), row(lambda_k1[j]), row(lambda_q2[j]),
                            row(lambda_k2[j]), colv(a_subln[j]), lam_init)
            mix_b = _attn_b(qtb, kb, vtb)
            w_out = w_out_ab[j].astype(BF16)
            x1 = _outproj(xf, [mix_a.reshape(t, A_V), mix_b.reshape(t, B_Q)], [w_out[:A_V], w_out[A_V:]])
        else:
            w_c = w_in_c[j].astype(BF16)
            qt, k, vt = _proj_odd(xf, row(attn_norm[i]), w_c[:, :C_MIX].T, w_c[:, C_MIX:2 * C_MIX],
                                  w_c[:, 2 * C_MIX:].T, batch, seq)
            o = _na(qt, k, vt, _na_bias(c_rel_bias[j], seq // GRID_W))
            x1 = _outproj(xf, [o.reshape(t, C_MIX)], [w_out_c[j].astype(BF16)])
        xf = _ffn(x1, row(ffn_norm[i]), w_ffn_up[i].astype(BF16), ffn_conv_w[i], row(ffn_conv_b[i]),
                  w_ffn_down[i].astype(BF16), row(ple_norm[i]), w_ple_gate[i].astype(BF16),
                  p[i].reshape(t, PLE_DIM), w_ple_proj[i].astype(BF16), row(final_norm), seq,
                  final=(i == depth - 1))
    return xf.reshape(batch, seq, D_MODEL)
```

```python
import functools
import math

import jax
import jax.numpy as jnp
from jax import lax
from jax.experimental import pallas as pl
from jax.experimental.pallas import tpu as pltpu

F32 = jnp.float32
BF16 = jnp.bfloat16

D_MODEL = 1024
HEAD_DIM = 64
A_HEADS = 4
B_Q_HEADS = 8
B_KV_HEADS = 2
C_HEADS = 16
ROPE_THETA = 500000.0
AXIAL_THETA = 10000.0
GRID_W = 64
NA_KH = 8
NA_KW = 16
D_FF = 2816
PLE_DIM = 256
EPS = 1e-6
SCALE = HEAD_DIM ** -0.5
LOG2E = math.log2(math.e)
Q_SCALE = SCALE * LOG2E

A_QK = A_HEADS * 2 * HEAD_DIM
A_V = A_HEADS * 2 * HEAD_DIM
B_Q = B_Q_HEADS * HEAD_DIM
B_KV = B_KV_HEADS * HEAD_DIM
EVEN_IN = 2 * A_QK + A_V + B_Q + 2 * B_KV
C_MIX = C_HEADS * HEAD_DIM

TM = 1024
TQ_A = 512
TQ_B = 256
TK_A = 512
TK_B = 256
VA_ROWS = 2 * HEAD_DIM + 16
VB_ROWS = HEAD_DIM + 16
SCORE_SLOTS = 2
TM_FFN = 256
TF = 256
FFN_SLOTS = 4
HALO = 8
NA_ROWS = 4
NA_WIN = NA_ROWS + NA_KH
VC_ROWS = 2 * HEAD_DIM + 16
NA_GROUPS_PER_TRIP = 16
VMEM_LIMIT = 52 * 1024 * 1024

_NT = (((1,), (1,)), ((), ()))


def _dot(a, b):
    return jnp.dot(a, b, preferred_element_type=F32)


def _rms(xf, g):
    return xf * lax.rsqrt(jnp.mean(xf * xf, axis=-1, keepdims=True) + EPS) * g


def _params(n_axes, limit=VMEM_LIMIT):
    return pltpu.CompilerParams(dimension_semantics=("arbitrary",) * n_axes, vmem_limit_bytes=limit)


def _resident(shape):
    nd = len(shape)
    return pl.BlockSpec(shape, lambda *_: (0,) * nd, pipeline_mode=pl.Buffered(1))


def _proj_even_kernel(x_ref, g_ref, wt_ref, ca_ref, sa_ref, cr_ref, sr_ref, cc_ref, sc_ref, qn_ref, kn_ref,
                      qta_ref, ka_ref, vta_ref, qtb_ref, kb_ref, vtb_ref):
    h = _rms(x_ref[...], g_ref[...]).astype(BF16)

    def proj_t(lo, hi):
        return lax.dot_general(wt_ref[lo:hi, :], h, _NT, preferred_element_type=F32)

    ca, sa = ca_ref[...], sa_ref[...]
    cr, sr = cr_ref[...], sr_ref[...]
    cc, sc = cc_ref[...], sc_ref[...]

    def rope_a(blk):
        x1, x2 = blk[0:8], blk[8:16]
        return jnp.concatenate([x1 * ca - x2 * sa, x2 * ca + x1 * sa, blk[16:]], axis=0)

    def norm_rope_b(blk, gain):
        n = blk * lax.rsqrt(jnp.mean(blk * blk, axis=0, keepdims=True) + EPS) * gain
        r1, r2, c1, c2 = n[0:16], n[16:32], n[32:48], n[48:64]
        return jnp.concatenate([r1 * cr - r2 * sr, r2 * cr + r1 * sr,
                                c1 * cc - c2 * sc, c2 * cc + c1 * sc], axis=0)

    o = 0
    aq = proj_t(o, o + A_QK)
    qta_ref[0] = (jnp.concatenate([rope_a(aq[m * 64:(m + 1) * 64]) for m in range(2 * A_HEADS)], axis=0)
                  * Q_SCALE).astype(BF16)
    o += A_QK
    ak = proj_t(o, o + A_QK)
    for hh in range(A_HEADS):
        kt = jnp.concatenate([rope_a(ak[(2 * hh + m) * 64:(2 * hh + m + 1) * 64]) for m in range(2)], axis=0)
        ka_ref[0, hh] = kt.T.astype(BF16)
    o += A_QK
    av = proj_t(o, o + A_V)
    pad_a = jnp.concatenate([jnp.ones((8, TM), F32), jnp.zeros((8, TM), F32)], axis=0)
    for hh in range(A_HEADS):
        vta_ref[0, hh] = jnp.concatenate([av[hh * 128:(hh + 1) * 128], pad_a], axis=0).astype(BF16)
    o += A_V
    bq = proj_t(o, o + B_Q)
    qn = qn_ref[...]
    qtb_ref[0] = (jnp.concatenate([norm_rope_b(bq[m * 64:(m + 1) * 64], qn) for m in range(B_Q_HEADS)], axis=0)
                  * Q_SCALE).astype(BF16)
    o += B_Q
    bk = proj_t(o, o + B_KV)
    kn = kn_ref[...]
    kbt = jnp.concatenate([norm_rope_b(bk[m * 64:(m + 1) * 64], kn) for m in range(B_KV_HEADS)], axis=0)
    kb_ref[0] = kbt.T.astype(BF16)
    o += B_KV
    bv = proj_t(o, o + B_KV)
    pad = jnp.concatenate([jnp.ones((8, TM), F32), jnp.zeros((VB_ROWS - HEAD_DIM - 8, TM), F32)], axis=0)
    for gg in range(B_KV_HEADS):
        vtb_ref[0, gg] = jnp.concatenate([bv[gg * 64:(gg + 1) * 64], pad], axis=0).astype(BF16)


def _proj_even(xf, g, wt, tabs, qn, kn, batch, seq):
    t = xf.shape[0]
    ns = seq // TM
    tok = lambda i: (i // ns, 0, i % ns)
    tab = lambda rows: pl.BlockSpec((rows, TM), lambda i: (0, i % ns))
    out_shape = (
        jax.ShapeDtypeStruct((batch, A_QK, seq), BF16),
        jax.ShapeDtypeStruct((batch, A_HEADS, seq, 128), BF16),
        jax.ShapeDtypeStruct((batch, A_HEADS, VA_ROWS, seq), BF16),
        jax.ShapeDtypeStruct((batch, B_Q, seq), BF16),
        jax.ShapeDtypeStruct((batch, seq, B_KV), BF16),
        jax.ShapeDtypeStruct((batch, B_KV_HEADS, VB_ROWS, seq), BF16),
    )
    return pl.pallas_call(
        _proj_even_kernel,
        grid=(t // TM,),
        in_specs=[pl.BlockSpec((TM, D_MODEL), lambda i: (i, 0)),
                  _resident((1, D_MODEL)),
                  _resident((EVEN_IN, D_MODEL)),
                  tab(8), tab(8), tab(16), tab(16), tab(16), tab(16),
                  _resident((HEAD_DIM, 1)), _resident((HEAD_DIM, 1))],
        out_specs=(pl.BlockSpec((1, A_QK, TM), tok),
                   pl.BlockSpec((1, A_HEADS, TM, 128), lambda i: (i // ns, 0, i % ns, 0)),
                   pl.BlockSpec((1, A_HEADS, VA_ROWS, TM), lambda i: (i // ns, 0, 0, i % ns)),
                   pl.BlockSpec((1, B_Q, TM), tok),
                   pl.BlockSpec((1, TM, B_KV), lambda i: (i // ns, i % ns, 0)),
                   pl.BlockSpec((1, B_KV_HEADS, VB_ROWS, TM), lambda i: (i // ns, 0, 0, i % ns))),
        out_shape=out_shape,
        compiler_params=_params(1),
        name="proj_even",
    )(xf, g, wt, *tabs, qn, kn)


def _flash_tiles(load_w, scores_fn, values_fn, finalize_fn, n_tiles, n_blocks, w_scr, s_scr, p_scr, a_scr, m_scr,
                 l_scr, acc_scr):
    n_slots = s_scr.shape[0]
    assert n_blocks % n_slots == 0

    def reset():
        m_scr[...] = jnp.full(m_scr.shape, -jnp.inf, F32)
        acc_scr[...] = jnp.zeros(acc_scr.shape, F32)
        if l_scr is not None:
            l_scr[...] = jnp.zeros(l_scr.shape, F32)

    def produce(j, slot, w):
        s_t = scores_fn(j, w)
        s_scr[slot] = s_t
        return jnp.max(s_t, axis=0, keepdims=True)

    def softmax(slot, blk_max):
        m_old = m_scr[...]
        m_new = jnp.maximum(m_old, blk_max)
        alpha = jnp.exp2(m_old - m_new)
        p = jnp.exp2(s_scr[slot] - m_new)
        if l_scr is not None:
            l_scr[...] = alpha * l_scr[...] + jnp.sum(p, axis=0, keepdims=True)
        p_scr[slot] = p.astype(BF16)
        a_scr[slot] = alpha
        m_scr[...] = m_new

    def accumulate(j, slot):
        acc_scr[...] = a_scr[slot] * acc_scr[...] + _dot(values_fn(j), p_scr[slot])

    def body(t, blk_max):
        cur = t % 2
        for u in range(n_blocks):
            if u + 1 < n_blocks:
                nxt = produce(u + 1, (u + 1) % n_slots, w_scr[cur])
            else:
                w_next = load_w(jnp.minimum(t + 1, n_tiles - 1))
                w_scr[1 - cur] = w_next
                nxt = produce(0, 0, w_next)
            softmax(u % n_slots, blk_max)
            if u >= 1:
                accumulate(u - 1, (u - 1) % n_slots)
            blk_max = nxt
        accumulate(n_blocks - 1, (n_blocks - 1) % n_slots)
        finalize_fn(t)
        reset()
        return blk_max

    reset()
    w_scr[0] = load_w(0)
    lax.fori_loop(0, n_tiles, body, produce(0, 0, w_scr[0]))


def _attn_a_kernel(q_ref, k_ref, v_ref, lq1_ref, lk1_ref, lq2_ref, lk2_ref, sub_ref, o_ref,
                   w_scr, s_scr, p_scr, a_scr, m_scr, acc_scr, *, lam_init, seq):
    tile = lambda t: pl.ds(pl.multiple_of(t * TQ_A, TQ_A), TQ_A)
    blk = lambda j: pl.ds(j * TK_A, TK_A)

    def load_w(t):
        qt = q_ref[0, :, tile(t)].astype(F32)
        row = lax.broadcasted_iota(jnp.int32, qt.shape, 0)
        zero = jnp.zeros_like(qt)
        return jnp.concatenate([jnp.where(row < HEAD_DIM, qt, zero), jnp.where(row >= HEAD_DIM, qt, zero)],
                               axis=1).astype(BF16)

    def finalize(t):
        o = acc_scr[0:128, :] / acc_scr[128:129, :]
        lam = (jnp.exp(jnp.sum(lq1_ref[...] * lk1_ref[...], axis=-1, keepdims=True))
               - jnp.exp(jnp.sum(lq2_ref[...] * lk2_ref[...], axis=-1, keepdims=True)) + lam_init)
        d = o[:, :TQ_A] - lam * o[:, TQ_A:]
        y = d * lax.rsqrt(jnp.mean(d * d, axis=0, keepdims=True) + EPS) * sub_ref[...]
        o_ref[0, tile(t), :] = (y * (1.0 - lam_init)).T.astype(BF16)

    _flash_tiles(load_w, lambda j, w: _dot(k_ref[0, 0, blk(j), :], w),
                 lambda j: v_ref[0, 0, :, blk(j)], finalize, seq // TQ_A, seq // TK_A,
                 w_scr, s_scr, p_scr, a_scr, m_scr, None, acc_scr)


def _attn_a(qta, ka, vta, lq1, lk1, lq2, lk2, subln, lam_init):
    batch, _, seq = qta.shape
    vec = _resident((1, HEAD_DIM))
    n = 2 * TQ_A
    return pl.pallas_call(
        functools.partial(_attn_a_kernel, lam_init=lam_init, seq=seq),
        grid=(batch, A_HEADS),
        in_specs=[pl.BlockSpec((1, 128, seq), lambda b, h: (b, h, 0)),
                  pl.BlockSpec((1, 1, seq, 128), lambda b, h: (b, h, 0, 0)),
                  pl.BlockSpec((1, 1, VA_ROWS, seq), lambda b, h: (b, h, 0, 0)),
                  vec, vec, vec, vec,
                  _resident((128, 1))],
        out_specs=pl.BlockSpec((1, seq, 128), lambda b, h: (b, 0, h)),
        out_shape=jax.ShapeDtypeStruct((batch, seq, A_V), BF16),
        scratch_shapes=[pltpu.VMEM((2, 128, n), BF16), pltpu.VMEM((SCORE_SLOTS, TK_A, n), F32),
                        pltpu.VMEM((SCORE_SLOTS, TK_A, n), BF16), pltpu.VMEM((SCORE_SLOTS, 1, n), F32),
                        pltpu.VMEM((1, n), F32), pltpu.VMEM((VA_ROWS, n), F32)],
        compiler_params=_params(2),
        name="attn_diff",
    )(qta, ka, vta, lq1, lk1, lq2, lk2, subln)


def _attn_b_kernel(q_ref, k_ref, v_ref, o_ref, w_scr, s_scr, p_scr, a_scr, m_scr, acc_scr, *, seq):
    g = pl.program_id(1)
    rep = B_Q_HEADS // B_KV_HEADS
    tile = lambda t: pl.ds(pl.multiple_of(t * TQ_B, TQ_B), TQ_B)
    blk = lambda j: pl.ds(j * TK_B, TK_B)

    def load_w(t):
        qt = q_ref[0, :, tile(t)].astype(F32)
        wq = jnp.concatenate([qt[r * 64:(r + 1) * 64] for r in range(rep)], axis=1)
        zero = jnp.zeros_like(wq)
        return jnp.concatenate([jnp.where(g == gg, wq, zero) for gg in range(B_KV_HEADS)], axis=0).astype(BF16)

    def finalize(t):
        o = acc_scr[0:HEAD_DIM, :] / acc_scr[HEAD_DIM:HEAD_DIM + 1, :]
        ot = jnp.concatenate([o[:, r * TQ_B:(r + 1) * TQ_B] for r in range(rep)], axis=0)
        o_ref[0, tile(t), :] = ot.T.astype(BF16)

    _flash_tiles(load_w, lambda j, w: _dot(k_ref[0, blk(j), :], w),
                 lambda j: v_ref[0, 0, :, blk(j)], finalize, seq // TQ_B, seq // TK_B,
                 w_scr, s_scr, p_scr, a_scr, m_scr, None, acc_scr)


def _attn_b(qtb, kb, vtb):
    batch, _, seq = qtb.shape
    rep = B_Q_HEADS // B_KV_HEADS
    n = rep * TQ_B
    return pl.pallas_call(
        functools.partial(_attn_b_kernel, seq=seq),
        grid=(batch, B_KV_HEADS),
        in_specs=[pl.BlockSpec((1, rep * HEAD_DIM, seq), lambda b, g: (b, g, 0)),
                  pl.BlockSpec((1, seq, B_KV), lambda b, g: (b, 0, 0)),
                  pl.BlockSpec((1, 1, VB_ROWS, seq), lambda b, g: (b, g, 0, 0))],
        out_specs=pl.BlockSpec((1, seq, rep * HEAD_DIM), lambda b, g: (b, 0, g)),
        out_shape=jax.ShapeDtypeStruct((batch, seq, B_Q), BF16),
        scratch_shapes=[pltpu.VMEM((2, B_KV, n), BF16), pltpu.VMEM((SCORE_SLOTS, TK_B, n), F32),
                        pltpu.VMEM((SCORE_SLOTS, TK_B, n), BF16), pltpu.VMEM((SCORE_SLOTS, 1, n), F32),
                        pltpu.VMEM((1, n), F32), pltpu.VMEM((VB_ROWS, n), F32)],
        compiler_params=_params(2),
        name="attn_gqa",
    )(qtb, kb, vtb)


def _proj_odd_kernel(x_ref, g_ref, wqt_ref, wk_ref, wvt_ref, qt_ref, k_ref, vt_ref):
    h = _rms(x_ref[...], g_ref[...]).astype(BF16)
    qt_ref[0] = (lax.dot_general(wqt_ref[...], h, _NT, preferred_element_type=F32) * Q_SCALE).astype(BF16)
    k_ref[0] = _dot(h, wk_ref[...]).astype(BF16)
    vt = lax.dot_general(wvt_ref[...], h, _NT, preferred_element_type=F32)
    pad = jnp.concatenate([jnp.ones((8, TM), F32), jnp.zeros((VC_ROWS - 2 * HEAD_DIM - 8, TM), F32)], axis=0)
    for pr in range(C_HEADS // 2):
        vt_ref[0, pr] = jnp.concatenate([vt[pr * 2 * HEAD_DIM:(pr + 1) * 2 * HEAD_DIM], pad], axis=0).astype(BF16)


def _proj_odd(xf, g, wqt, wk, wvt, batch, seq):
    t = xf.shape[0]
    ns = seq // TM
    tok_t = pl.BlockSpec((1, C_MIX, TM), lambda i: (i // ns, 0, i % ns))
    sq = _resident((C_MIX, D_MODEL))
    return pl.pallas_call(
        _proj_odd_kernel,
        grid=(t // TM,),
        in_specs=[pl.BlockSpec((TM, D_MODEL), lambda i: (i, 0)), _resident((1, D_MODEL)),
                  sq, _resident((D_MODEL, C_MIX)), sq],
        out_specs=(tok_t, pl.BlockSpec((1, TM, C_MIX), lambda i: (i // ns, i % ns, 0)),
                   pl.BlockSpec((1, C_HEADS // 2, VC_ROWS, TM), lambda i: (i // ns, 0, 0, i % ns))),
        out_shape=(jax.ShapeDtypeStruct((batch, C_MIX, seq), BF16),
                   jax.ShapeDtypeStruct((batch, seq, C_MIX), BF16),
                   jax.ShapeDtypeStruct((batch, C_HEADS // 2, VC_ROWS, seq), BF16)),
        compiler_params=_params(1),
        name="proj_odd",
    )(xf, g, wqt, wk, wvt)


def _na_window_start(r0, rows):
    return jnp.clip(r0 - NA_KH // 2, 0, rows - NA_WIN)


def _na_bias_kernel(tab_ref, o_ref, *, rows):
    h = pl.program_id(0)
    shp = (GRID_W, 2 * GRID_W)
    kc = lax.broadcasted_iota(jnp.int32, shp, 0)
    lane = lax.broadcasted_iota(jnp.int32, shp, 1)
    qc = lane & (GRID_W - 1)
    cs = jnp.clip(qc - NA_KW // 2, 0, GRID_W - NA_KW)
    col_valid = (kc >= cs) & (kc < cs + NA_KW)
    dci = jnp.clip(kc - qc + NA_KW - 1, 0, 2 * NA_KW - 2)
    neg = jnp.full(shp, -jnp.inf, F32)
    n_dc = 2 * NA_KW - 1
    n_dr = 2 * NA_KH - 1
    planes = []
    for dr in range(n_dr):
        acc = jnp.zeros(shp, F32)
        for dd in range(n_dc):
            acc = jnp.where(dci == dd, tab_ref[h, dr * n_dc + dd], acc)
        planes.append(jnp.where(col_valid, acc * LOG2E, neg))
    clamp = lambda v, lo, hi: max(lo, min(v, hi))
    for c, r0 in enumerate((0, NA_ROWS, rows - NA_ROWS)):
        ws = clamp(r0 - NA_KH // 2, 0, rows - NA_WIN)
        for j in range(NA_WIN):
            kr = ws + j
            for a0 in range(0, NA_ROWS, 2):
                halves = []
                for r in (r0 + a0, r0 + a0 + 1):
                    rs = clamp(r - NA_KH // 2, 0, rows - NA_KH)
                    halves.append(planes[kr - r + NA_KH - 1] if rs <= kr < rs + NA_KH else neg)
                o_ref[0, c, j * GRID_W:(j + 1) * GRID_W, a0 * GRID_W:(a0 + 2) * GRID_W] = (
                    jnp.where(lane < GRID_W, halves[0], halves[1]))


def _na_bias(rel_bias, rows):
    tab = rel_bias.reshape(C_HEADS, -1)
    shp = (3, NA_WIN * GRID_W, NA_ROWS * GRID_W)
    return pl.pallas_call(
        functools.partial(_na_bias_kernel, rows=rows),
        grid=(C_HEADS,),
        in_specs=[pl.BlockSpec(memory_space=pltpu.SMEM)],
        out_specs=pl.BlockSpec((1,) + shp, lambda h: (h, 0, 0, 0)),
        out_shape=jax.ShapeDtypeStruct((C_HEADS,) + shp, F32),
        compiler_params=_params(1),
        name="na_bias_tiles",
    )(tab)


def _na_kernel(q_ref, k_ref, v_ref, bias_ref, o_ref, s_scr, *, rows):
    nk = NA_WIN * GRID_W
    nq = NA_ROWS * GRID_W
    n_groups = rows // NA_ROWS
    hrow = lax.broadcasted_iota(jnp.int32, (2 * HEAD_DIM, nq), 0) // HEAD_DIM

    def window(g):
        ws = _na_window_start(g * NA_ROWS, rows)
        return pl.ds(pl.multiple_of(ws * GRID_W, 2 * GRID_W), nk)

    def produce(g, slot):
        qt = q_ref[0, :, pl.ds(pl.multiple_of(g * nq, nq), nq)].astype(F32)
        zero = jnp.zeros_like(qt)
        w = jnp.concatenate([jnp.where(hrow == hh, qt, zero) for hh in range(2)], axis=1).astype(BF16)
        case = jnp.where(g == 0, 0, jnp.where(g == n_groups - 1, 2, 1))
        bias = jnp.concatenate([bias_ref[hh, case] for hh in range(2)], axis=1)
        s_t = _dot(k_ref[0, window(g), :], w) + bias
        s_scr[slot] = s_t
        return jnp.max(s_t, axis=0, keepdims=True)

    def consume(g, slot, blk_max):
        p = jnp.exp2(s_scr[slot] - blk_max)
        ot = _dot(v_ref[0, 0, :, window(g)], p.astype(BF16))
        ot = ot[0:2 * HEAD_DIM] / ot[2 * HEAD_DIM:2 * HEAD_DIM + 1]
        o = jnp.concatenate([ot[hh * HEAD_DIM:(hh + 1) * HEAD_DIM, hh * nq:(hh + 1) * nq] for hh in range(2)],
                            axis=0)
        o_ref[0, pl.ds(pl.multiple_of(g * nq, nq), nq), :] = o.T.astype(BF16)

    per_trip = min(NA_GROUPS_PER_TRIP, n_groups)
    assert per_trip % 2 == 0 and n_groups % per_trip == 0

    def body(i, blk_max):
        g = per_trip * i
        for u in range(per_trip):
            nxt = produce(jnp.minimum(g + u + 1, n_groups - 1), (u + 1) % 2)
            consume(g + u, u % 2, blk_max)
            blk_max = nxt
        return blk_max

    lax.fori_loop(0, n_groups // per_trip, body, produce(0, 0))


def _na(qt, k, vt, bias):
    batch, seq, _ = k.shape
    rows = seq // GRID_W
    assert rows % 2 == 0 and rows >= NA_WIN + NA_ROWS
    nq = NA_ROWS * GRID_W
    pairs = C_HEADS // 2
    nat = pl.BlockSpec((1, seq, 2 * HEAD_DIM), lambda b, h: (b, 0, h))
    return pl.pallas_call(
        functools.partial(_na_kernel, rows=rows),
        grid=(batch, pairs),
        in_specs=[pl.BlockSpec((1, 2 * HEAD_DIM, seq), lambda b, h: (b, h, 0)),
                  nat,
                  pl.BlockSpec((1, 1, VC_ROWS, seq), lambda b, h: (b, h, 0, 0)),
                  pl.BlockSpec((2, 3, NA_WIN * GRID_W, nq), lambda b, h: (h, 0, 0, 0))],
        out_specs=nat,
        out_shape=jax.ShapeDtypeStruct((batch, seq, C_MIX), BF16),
        scratch_shapes=[pltpu.VMEM((2, NA_WIN * GRID_W, 2 * nq), F32)],
        compiler_params=_params(2),
        name="attn_nbr",
    )(qt, k, vt, bias)


def _outproj_kernel(*refs):
    n = (len(refs) - 2) // 2
    x_ref, o_ref = refs[0], refs[-1]
    y = x_ref[...]
    for a_ref, w_ref in zip(refs[1:1 + n], refs[1 + n:1 + 2 * n]):
        y = y + _dot(a_ref[...], w_ref[...])
    o_ref[...] = y


def _outproj(xf, mixes, weights):
    t = xf.shape[0]
    tile = lambda c: pl.BlockSpec((TM, c), lambda i: (i, 0))
    return pl.pallas_call(
        _outproj_kernel,
        grid=(t // TM,),
        in_specs=[tile(D_MODEL)] + [tile(m.shape[1]) for m in mixes] + [_resident(w.shape) for w in weights],
        out_specs=tile(D_MODEL),
        out_shape=jax.ShapeDtypeStruct((t, D_MODEL), F32),
        compiler_params=_params(1),
        name="out_proj",
    )(xf, *mixes, *weights)


def _ffn_kernel(xm_ref, xp_ref, xn_ref, g_ref, wu_ref, cw_ref, cb_ref, wd_ref, pg_ref, wg_ref, p_ref, wp_ref,
                fg_ref, o_ref, h_scr, acc_scr, u_scr, *, tiles_per_seq, final):
    i = pl.program_id(0)
    g = g_ref[...]
    xm = xm_ref[...]
    pos = i % tiles_per_seq
    hp = jnp.where(pos == 0, 0.0, _rms(xp_ref[...], g))
    hn = jnp.where(pos == tiles_per_seq - 1, 0.0, _rms(xn_ref[...], g))
    h_scr[...] = jnp.concatenate([hp, _rms(xm, g), hn], axis=0).astype(BF16)
    acc_scr[...] = jnp.zeros(acc_scr.shape, F32)

    def conv(u, col):
        cw = cw_ref[:, pl.ds(col, TF)]
        rows = u.shape[0]
        y = (pltpu.roll(u, 1, 0) * cw[0:1] + u * cw[1:2] + pltpu.roll(u, rows - 1, 0) * cw[2:3]
             + cb_ref[:, pl.ds(col, TF)])
        return y[HALO:HALO + TM_FFN]

    def up(c, slot):
        h = h_scr[...]
        u_scr[slot, 0] = _dot(h, wu_ref[:, pl.ds(c * TF, TF)])
        u_scr[slot, 1] = _dot(h, wu_ref[:, pl.ds(D_FF + c * TF, TF)])

    def act_down(c, slot):
        gate = conv(u_scr[slot, 0], c * TF)
        val = conv(u_scr[slot, 1], D_FF + c * TF)
        act = gate * jax.nn.sigmoid(gate) * val
        acc_scr[...] += _dot(act.astype(BF16), wd_ref[pl.ds(c * TF, TF), :])

    n_chunks = D_FF // TF
    n_slots = u_scr.shape[0]
    for c in range(min(n_slots - 1, n_chunks)):
        up(c, c)
    for c in range(n_chunks):
        ahead = c + n_slots - 1
        if ahead < n_chunks:
            up(ahead, ahead % n_slots)
        act_down(c, c % n_slots)

    x2 = xm + acc_scr[...]
    gate = jax.nn.sigmoid(_dot(_rms(x2, pg_ref[...]).astype(BF16), wg_ref[...]))
    x3 = x2 + gate * _dot(p_ref[...].astype(BF16), wp_ref[...])
    if final:
        x3 = _rms(x3, fg_ref[...])
    o_ref[...] = x3


def _ffn(xf, g, wu, cw, cb, wd, pg, wg, p, wp, fg, seq, final):
    t = xf.shape[0]
    tps = seq // TM_FFN
    nb = TM_FFN // HALO
    last = t // HALO - 1
    return pl.pallas_call(
        functools.partial(_ffn_kernel, tiles_per_seq=tps, final=final),
        grid=(t // TM_FFN,),
        in_specs=[pl.BlockSpec((TM_FFN, D_MODEL), lambda i: (i, 0)),
                  pl.BlockSpec((HALO, D_MODEL), lambda i: (jnp.maximum(i * nb - 1, 0), 0)),
                  pl.BlockSpec((HALO, D_MODEL), lambda i: (jnp.minimum((i + 1) * nb, last), 0)),
                  _resident((1, D_MODEL)),
                  _resident((D_MODEL, 2 * D_FF)),
                  _resident((3, 2 * D_FF)), _resident((1, 2 * D_FF)),
                  _resident((D_FF, D_MODEL)),
                  _resident((1, D_MODEL)),
                  _resident((D_MODEL, D_MODEL)),
                  pl.BlockSpec((TM_FFN, PLE_DIM), lambda i: (i, 0)),
                  _resident((PLE_DIM, D_MODEL)),
                  _resident((1, D_MODEL))],
        out_specs=pl.BlockSpec((TM_FFN, D_MODEL), lambda i: (i, 0)),
        out_shape=jax.ShapeDtypeStruct((t, D_MODEL), F32),
        scratch_shapes=[pltpu.VMEM((TM_FFN + 2 * HALO, D_MODEL), BF16), pltpu.VMEM((TM_FFN, D_MODEL), F32),
                        pltpu.VMEM((FFN_SLOTS, 2, TM_FFN + 2 * HALO, TF), F32)],
        compiler_params=_params(1),
        name="convffn_ple",
    )(xf, xf, xf, g, wu, cw, cb, wd, pg, wg, p, wp, fg)


def _rope_tables(seq):
    pos = jnp.arange(seq, dtype=jnp.int32)

    def tab(p, half, theta):
        inv = theta ** (-jnp.arange(half, dtype=F32) / half)
        ang = inv[:, None] * p.astype(F32)[None, :]
        return jnp.cos(ang), jnp.sin(ang)

    ca, sa = tab(pos, HEAD_DIM // 8, ROPE_THETA)
    cr, sr = tab(pos // GRID_W, HEAD_DIM // 4, AXIAL_THETA)
    cc, sc = tab(pos % GRID_W, HEAD_DIM // 4, AXIAL_THETA)
    return ca, sa, cr, sr, cc, sc


def kernel(x, p, attn_norm, w_in_ab, lambda_q1, lambda_k1, lambda_q2, lambda_k2, a_subln, b_q_norm, b_k_norm,
           w_out_ab, w_in_c, c_rel_bias, w_out_c, ffn_norm, w_ffn_up, ffn_conv_w, ffn_conv_b, w_ffn_down,
           ple_norm, w_ple_gate, w_ple_proj, final_norm):
    batch, seq, _ = x.shape
    depth = attn_norm.shape[0]
    t = batch * seq
    xf = x.reshape(t, D_MODEL)
    tabs = _rope_tables(seq)
    row = lambda v: v.reshape(1, -1)
    colv = lambda v: v.reshape(-1, 1)
    for i in range(depth):
        j = i // 2
        if i % 2 == 0:
            lam_init = 0.8 - 0.6 * math.exp(-0.3 * i)
            wt = w_in_ab[j].T.astype(BF16)
            qta, ka, vta, qtb, kb, vtb = _proj_even(xf, row(attn_norm[i]), wt, tabs, colv(b_q_norm[j]),
                                                    colv(b_k_norm[j]), batch, seq)
            mix_a = _attn_a(qta, ka, vta, row(lambda_q1[j]), row(lambda_k1[j]), row(lambda_q2[j]),
                            row(lambda_k2[j]), colv(a_subln[j]), lam_init)
            mix_b = _attn_b(qtb, kb, vtb)
            w_out = w_out_ab[j].astype(BF16)
            x1 = _outproj(xf, [mix_a.reshape(t, A_V), mix_b.reshape(t, B_Q)], [w_out[:A_V], w_out[A_V:]])
        else:
            w_c = w_in_c[j].astype(BF16)
            qt, k, vt = _proj_odd(xf, row(attn_norm[i]), w_c[:, :C_MIX].T, w_c[:, C_MIX:2 * C_MIX],
                                  w_c[:, 2 * C_MIX:].T, batch, seq)
            o = _na(qt, k, vt, _na_bias(c_rel_bias[j], seq // GRID_W))
            x1 = _outproj(xf, [o.reshape(t, C_MIX)], [w_out_c[j].astype(BF16)])
        xf = _ffn(x1, row(ffn_norm[i]), w_ffn_up[i].astype(BF16), ffn_conv_w[i], row(ffn_conv_b[i]),
                  w_ffn_down[i].astype(BF16), row(ple_norm[i]), w_ple_gate[i].astype(BF16),
                  p[i].reshape(t, PLE_DIM), w_ple_proj[i].astype(BF16), row(final_norm), seq,
                  final=(i == depth - 1))
    return xf.reshape(batch, seq, D_MODEL)
```

```python
import functools
import math

import jax
import jax.numpy as jnp
from jax import lax
from jax.experimental import pallas as pl
from jax.experimental.pallas import tpu as pltpu

F32 = jnp.float32
BF16 = jnp.bfloat16

D_MODEL = 1024
HEAD_DIM = 64
A_HEADS = 4
B_Q_HEADS = 8
B_KV_HEADS = 2
C_HEADS = 16
ROPE_THETA = 500000.0
AXIAL_THETA = 10000.0
GRID_W = 64
NA_KH = 8
NA_KW = 16
D_FF = 2816
PLE_DIM = 256
EPS = 1e-6
SCALE = HEAD_DIM ** -0.5
LOG2E = math.log2(math.e)
Q_SCALE = SCALE * LOG2E

A_QK = A_HEADS * 2 * HEAD_DIM
A_V = A_HEADS * 2 * HEAD_DIM
B_Q = B_Q_HEADS * HEAD_DIM
B_KV = B_KV_HEADS * HEAD_DIM
EVEN_IN = 2 * A_QK + A_V + B_Q + 2 * B_KV
C_MIX = C_HEADS * HEAD_DIM

TM = 1024
TQ_A = 512
TQ_B = 256
TK_A = 512
TK_B = 256
VA_ROWS = 2 * HEAD_DIM + 16
VB_ROWS = HEAD_DIM + 16
SCORE_SLOTS = 2
TM_FFN = 256
TF = 256
FFN_SLOTS = 4
HALO = 8
NA_ROWS = 4
NA_WIN = NA_ROWS + NA_KH
VC_ROWS = 2 * HEAD_DIM + 16
NA_GROUPS_PER_TRIP = 16
VMEM_LIMIT = 52 * 1024 * 1024

_NT = (((1,), (1,)), ((), ()))


def _dot(a, b):
    return jnp.dot(a, b, preferred_element_type=F32)


def _rms(xf, g):
    return xf * lax.rsqrt(jnp.mean(xf * xf, axis=-1, keepdims=True) + EPS) * g


def _params(n_axes, limit=VMEM_LIMIT):
    return pltpu.CompilerParams(dimension_semantics=("arbitrary",) * n_axes, vmem_limit_bytes=limit)


def _resident(shape):
    nd = len(shape)
    return pl.BlockSpec(shape, lambda *_: (0,) * nd, pipeline_mode=pl.Buffered(1))


def _proj_even_kernel(x_ref, g_ref, wt_ref, ca_ref, sa_ref, cr_ref, sr_ref, cc_ref, sc_ref, qn_ref, kn_ref,
                      qta_ref, ka_ref, vta_ref, qtb_ref, kb_ref, vtb_ref):
    h = _rms(x_ref[...], g_ref[...]).astype(BF16)

    def proj_t(lo, hi):
        return lax.dot_general(wt_ref[lo:hi, :], h, _NT, preferred_element_type=F32)

    ca, sa = ca_ref[...], sa_ref[...]
    cr, sr = cr_ref[...], sr_ref[...]
    cc, sc = cc_ref[...], sc_ref[...]

    def rope_a(blk):
        x1, x2 = blk[0:8], blk[8:16]
        return jnp.concatenate([x1 * ca - x2 * sa, x2 * ca + x1 * sa, blk[16:]], axis=0)

    def norm_rope_b(blk, gain):
        n = blk * lax.rsqrt(jnp.mean(blk * blk, axis=0, keepdims=True) + EPS) * gain
        r1, r2, c1, c2 = n[0:16], n[16:32], n[32:48], n[48:64]
        return jnp.concatenate([r1 * cr - r2 * sr, r2 * cr + r1 * sr,
                                c1 * cc - c2 * sc, c2 * cc + c1 * sc], axis=0)

    o = 0
    aq = proj_t(o, o + A_QK)
    qta_ref[0] = (jnp.concatenate([rope_a(aq[m * 64:(m + 1) * 64]) for m in range(2 * A_HEADS)], axis=0)
                  * Q_SCALE).astype(BF16)
    o += A_QK
    ak = proj_t(o, o + A_QK)
    for hh in range(A_HEADS):
        kt = jnp.concatenate([rope_a(ak[(2 * hh + m) * 64:(2 * hh + m + 1) * 64]) for m in range(2)], axis=0)
        ka_ref[0, hh] = kt.T.astype(BF16)
    o += A_QK
    av = proj_t(o, o + A_V)
    pad_a = jnp.concatenate([jnp.ones((8, TM), F32), jnp.zeros((8, TM), F32)], axis=0)
    for hh in range(A_HEADS):
        vta_ref[0, hh] = jnp.concatenate([av[hh * 128:(hh + 1) * 128], pad_a], axis=0).astype(BF16)
    o += A_V
    bq = proj_t(o, o + B_Q)
    qn = qn_ref[...]
    qtb_ref[0] = (jnp.concatenate([norm_rope_b(bq[m * 64:(m + 1) * 64], qn) for m in range(B_Q_HEADS)], axis=0)
                  * Q_SCALE).astype(BF16)
    o += B_Q
    bk = proj_t(o, o + B_KV)
    kn = kn_ref[...]
    kbt = jnp.concatenate([norm_rope_b(bk[m * 64:(m + 1) * 64], kn) for m in range(B_KV_HEADS)], axis=0)
    kb_ref[0] = kbt.T.astype(BF16)
    o += B_KV
    bv = proj_t(o, o + B_KV)
    pad = jnp.concatenate([jnp.ones((8, TM), F32), jnp.zeros((VB_ROWS - HEAD_DIM - 8, TM), F32)], axis=0)
    for gg in range(B_KV_HEADS):
        vtb_ref[0, gg] = jnp.concatenate([bv[gg * 64:(gg + 1) * 64], pad], axis=0).astype(BF16)


def _proj_even(xf, g, wt, tabs, qn, kn, batch, seq):
    t = xf.shape[0]
    ns = seq // TM
    tok = lambda i: (i // ns, 0, i % ns)
    tab = lambda rows: pl.BlockSpec((rows, TM), lambda i: (0, i % ns))
    out_shape = (
        jax.ShapeDtypeStruct((batch, A_QK, seq), BF16),
        jax.ShapeDtypeStruct((batch, A_HEADS, seq, 128), BF16),
        jax.ShapeDtypeStruct((batch, A_HEADS, VA_ROWS, seq), BF16),
        jax.ShapeDtypeStruct((batch, B_Q, seq), BF16),
        jax.ShapeDtypeStruct((batch, seq, B_KV), BF16),
        jax.ShapeDtypeStruct((batch, B_KV_HEADS, VB_ROWS, seq), BF16),
    )
    return pl.pallas_call(
        _proj_even_kernel,
        grid=(t // TM,),
        in_specs=[pl.BlockSpec((TM, D_MODEL), lambda i: (i, 0)),
                  _resident((1, D_MODEL)),
                  _resident((EVEN_IN, D_MODEL)),
                  tab(8), tab(8), tab(16), tab(16), tab(16), tab(16),
                  _resident((HEAD_DIM, 1)), _resident((HEAD_DIM, 1))],
        out_specs=(pl.BlockSpec((1, A_QK, TM), tok),
                   pl.BlockSpec((1, A_HEADS, TM, 128), lambda i: (i // ns, 0, i % ns, 0)),
                   pl.BlockSpec((1, A_HEADS, VA_ROWS, TM), lambda i: (i // ns, 0, 0, i % ns)),
                   pl.BlockSpec((1, B_Q, TM), tok),
                   pl.BlockSpec((1, TM, B_KV), lambda i: (i // ns, i % ns, 0)),
                   pl.BlockSpec((1, B_KV_HEADS, VB_ROWS, TM), lambda i: (i // ns, 0, 0, i % ns))),
        out_shape=out_shape,
        compiler_params=_params(1),
        name="proj_even",
    )(xf, g, wt, *tabs, qn, kn)


def _flash_tiles(load_w, scores_fn, values_fn, finalize_fn, n_tiles, n_blocks, w_scr, s_scr, p_scr, a_scr, m_scr,
                 acc_scr):
    n_slots = s_scr.shape[0]
    assert n_blocks % n_slots == 0

    def reset():
        m_scr[...] = jnp.full(m_scr.shape, -jnp.inf, F32)
        acc_scr[...] = jnp.zeros(acc_scr.shape, F32)

    def produce(j, slot, w):
        s_t = scores_fn(j, w)
        s_scr[slot] = s_t
        return jnp.max(s_t, axis=0, keepdims=True)

    def softmax(slot, blk_max):
        m_old = m_scr[...]
        m_new = jnp.maximum(m_old, blk_max)
        alpha = jnp.exp2(m_old - m_new)
        p_scr[slot] = jnp.exp2(s_scr[slot] - m_new).astype(BF16)
        a_scr[slot] = alpha
        m_scr[...] = m_new

    def accumulate(j, slot):
        acc_scr[...] = a_scr[slot] * acc_scr[...] + _dot(values_fn(j), p_scr[slot])

    def body(t, blk_max):
        cur = t % 2
        for u in range(n_blocks):
            if u + 1 < n_blocks:
                nxt = produce(u + 1, (u + 1) % n_slots, w_scr[cur])
            else:
                w_next = load_w(jnp.minimum(t + 1, n_tiles - 1))
                w_scr[1 - cur] = w_next
                nxt = produce(0, 0, w_next)
            softmax(u % n_slots, blk_max)
            if u >= 1:
                accumulate(u - 1, (u - 1) % n_slots)
            blk_max = nxt
        accumulate(n_blocks - 1, (n_blocks - 1) % n_slots)
        finalize_fn(t)
        reset()
        return blk_max

    reset()
    w_scr[0] = load_w(0)
    lax.fori_loop(0, n_tiles, body, produce(0, 0, w_scr[0]))


def _attn_a_kernel(q_ref, k_ref, v_ref, lq1_ref, lk1_ref, lq2_ref, lk2_ref, sub_ref, o_ref,
                   w_scr, s_scr, p_scr, a_scr, m_scr, acc_scr, *, lam_init, seq):
    tile = lambda t: pl.ds(pl.multiple_of(t * TQ_A, TQ_A), TQ_A)
    blk = lambda j: pl.ds(j * TK_A, TK_A)

    def load_w(t):
        qt = q_ref[0, :, tile(t)].astype(F32)
        row = lax.broadcasted_iota(jnp.int32, qt.shape, 0)
        zero = jnp.zeros_like(qt)
        return jnp.concatenate([jnp.where(row < HEAD_DIM, qt, zero), jnp.where(row >= HEAD_DIM, qt, zero)],
                               axis=1).astype(BF16)

    def finalize(t):
        o = acc_scr[0:128, :] / acc_scr[128:129, :]
        lam = (jnp.exp(jnp.sum(lq1_ref[...] * lk1_ref[...], axis=-1, keepdims=True))
               - jnp.exp(jnp.sum(lq2_ref[...] * lk2_ref[...], axis=-1, keepdims=True)) + lam_init)
        d = o[:, :TQ_A] - lam * o[:, TQ_A:]
        y = d * lax.rsqrt(jnp.mean(d * d, axis=0, keepdims=True) + EPS) * sub_ref[...]
        o_ref[0, tile(t), :] = (y * (1.0 - lam_init)).T.astype(BF16)

    _flash_tiles(load_w, lambda j, w: _dot(k_ref[0, 0, blk(j), :], w),
                 lambda j: v_ref[0, 0, :, blk(j)], finalize, seq // TQ_A, seq // TK_A,
                 w_scr, s_scr, p_scr, a_scr, m_scr, acc_scr)


def _attn_a(qta, ka, vta, lq1, lk1, lq2, lk2, subln, lam_init):
    batch, _, seq = qta.shape
    vec = _resident((1, HEAD_DIM))
    n = 2 * TQ_A
    return pl.pallas_call(
        functools.partial(_attn_a_kernel, lam_init=lam_init, seq=seq),
        grid=(batch, A_HEADS),
        in_specs=[pl.BlockSpec((1, 128, seq), lambda b, h: (b, h, 0)),
                  pl.BlockSpec((1, 1, seq, 128), lambda b, h: (b, h, 0, 0)),
                  pl.BlockSpec((1, 1, VA_ROWS, seq), lambda b, h: (b, h, 0, 0)),
                  vec, vec, vec, vec,
                  _resident((128, 1))],
        out_specs=pl.BlockSpec((1, seq, 128), lambda b, h: (b, 0, h)),
        out_shape=jax.ShapeDtypeStruct((batch, seq, A_V), BF16),
        scratch_shapes=[pltpu.VMEM((2, 128, n), BF16), pltpu.VMEM((SCORE_SLOTS, TK_A, n), F32),
                        pltpu.VMEM((SCORE_SLOTS, TK_A, n), BF16), pltpu.VMEM((SCORE_SLOTS, 1, n), F32),
                        pltpu.VMEM((1, n), F32), pltpu.VMEM((VA_ROWS, n), F32)],
        compiler_params=_params(2),
        name="attn_diff",
    )(qta, ka, vta, lq1, lk1, lq2, lk2, subln)


def _attn_b_kernel(q_ref, k_ref, v_ref, o_ref, w_scr, s_scr, p_scr, a_scr, m_scr, acc_scr, *, seq):
    g = pl.program_id(1)
    rep = B_Q_HEADS // B_KV_HEADS
    tile = lambda t: pl.ds(pl.multiple_of(t * TQ_B, TQ_B), TQ_B)
    blk = lambda j: pl.ds(j * TK_B, TK_B)

    def load_w(t):
        qt = q_ref[0, :, tile(t)].astype(F32)
        wq = jnp.concatenate([qt[r * 64:(r + 1) * 64] for r in range(rep)], axis=1)
        zero = jnp.zeros_like(wq)
        return jnp.concatenate([jnp.where(g == gg, wq, zero) for gg in range(B_KV_HEADS)], axis=0).astype(BF16)

    def finalize(t):
        o = acc_scr[0:HEAD_DIM, :] / acc_scr[HEAD_DIM:HEAD_DIM + 1, :]
        ot = jnp.concatenate([o[:, r * TQ_B:(r + 1) * TQ_B] for r in range(rep)], axis=0)
        o_ref[0, tile(t), :] = ot.T.astype(BF16)

    _flash_tiles(load_w, lambda j, w: _dot(k_ref[0, blk(j), :], w),
                 lambda j: v_ref[0, 0, :, blk(j)], finalize, seq // TQ_B, seq // TK_B,
                 w_scr, s_scr, p_scr, a_scr, m_scr, acc_scr)


def _attn_b(qtb, kb, vtb):
    batch, _, seq = qtb.shape
    rep = B_Q_HEADS // B_KV_HEADS
    n = rep * TQ_B
    return pl.pallas_call(
        functools.partial(_attn_b_kernel, seq=seq),
        grid=(batch, B_KV_HEADS),
        in_specs=[pl.BlockSpec((1, rep * HEAD_DIM, seq), lambda b, g: (b, g, 0)),
                  pl.BlockSpec((1, seq, B_KV), lambda b, g: (b, 0, 0)),
                  pl.BlockSpec((1, 1, VB_ROWS, seq), lambda b, g: (b, g, 0, 0))],
        out_specs=pl.BlockSpec((1, seq, rep * HEAD_DIM), lambda b, g: (b, 0, g)),
        out_shape=jax.ShapeDtypeStruct((batch, seq, B_Q), BF16),
        scratch_shapes=[pltpu.VMEM((2, B_KV, n), BF16), pltpu.VMEM((SCORE_SLOTS, TK_B, n), F32),
                        pltpu.VMEM((SCORE_SLOTS, TK_B, n), BF16), pltpu.VMEM((SCORE_SLOTS, 1, n), F32),
                        pltpu.VMEM((1, n), F32), pltpu.VMEM((VB_ROWS, n), F32)],
        compiler_params=_params(2),
        name="attn_gqa",
    )(qtb, kb, vtb)


def _proj_odd_kernel(x_ref, g_ref, wqt_ref, wk_ref, wvt_ref, qt_ref, k_ref, vt_ref):
    h = _rms(x_ref[...], g_ref[...]).astype(BF16)
    qt_ref[0] = (lax.dot_general(wqt_ref[...], h, _NT, preferred_element_type=F32) * Q_SCALE).astype(BF16)
    k_ref[0] = _dot(h, wk_ref[...]).astype(BF16)
    vt = lax.dot_general(wvt_ref[...], h, _NT, preferred_element_type=F32)
    pad = jnp.concatenate([jnp.ones((8, TM), F32), jnp.zeros((VC_ROWS - 2 * HEAD_DIM - 8, TM), F32)], axis=0)
    for pr in range(C_HEADS // 2):
        vt_ref[0, pr] = jnp.concatenate([vt[pr * 2 * HEAD_DIM:(pr + 1) * 2 * HEAD_DIM], pad], axis=0).astype(BF16)


def _proj_odd(xf, g, wqt, wk, wvt, batch, seq):
    t = xf.shape[0]
    ns = seq // TM
    tok_t = pl.BlockSpec((1, C_MIX, TM), lambda i: (i // ns, 0, i % ns))
    sq = _resident((C_MIX, D_MODEL))
    return pl.pallas_call(
        _proj_odd_kernel,
        grid=(t // TM,),
        in_specs=[pl.BlockSpec((TM, D_MODEL), lambda i: (i, 0)), _resident((1, D_MODEL)),
                  sq, _resident((D_MODEL, C_MIX)), sq],
        out_specs=(tok_t, pl.BlockSpec((1, TM, C_MIX), lambda i: (i // ns, i % ns, 0)),
                   pl.BlockSpec((1, C_HEADS // 2, VC_ROWS, TM), lambda i: (i // ns, 0, 0, i % ns))),
        out_shape=(jax.ShapeDtypeStruct((batch, C_MIX, seq), BF16),
                   jax.ShapeDtypeStruct((batch, seq, C_MIX), BF16),
                   jax.ShapeDtypeStruct((batch, C_HEADS // 2, VC_ROWS, seq), BF16)),
        compiler_params=_params(1),
        name="proj_odd",
    )(xf, g, wqt, wk, wvt)


def _na_window_start(r0, rows):
    return jnp.clip(r0 - NA_KH // 2, 0, rows - NA_WIN)


def _na_bias_kernel(tab_ref, o_ref, *, rows):
    h = pl.program_id(0)
    shp = (GRID_W, 2 * GRID_W)
    kc = lax.broadcasted_iota(jnp.int32, shp, 0)
    lane = lax.broadcasted_iota(jnp.int32, shp, 1)
    qc = lane & (GRID_W - 1)
    cs = jnp.clip(qc - NA_KW // 2, 0, GRID_W - NA_KW)
    col_valid = (kc >= cs) & (kc < cs + NA_KW)
    dci = jnp.clip(kc - qc + NA_KW - 1, 0, 2 * NA_KW - 2)
    neg = jnp.full(shp, -jnp.inf, F32)
    n_dc = 2 * NA_KW - 1
    n_dr = 2 * NA_KH - 1
    planes = []
    for dr in range(n_dr):
        acc = jnp.zeros(shp, F32)
        for dd in range(n_dc):
            acc = jnp.where(dci == dd, tab_ref[h, dr * n_dc + dd], acc)
        planes.append(jnp.where(col_valid, acc * LOG2E, neg))
    clamp = lambda v, lo, hi: max(lo, min(v, hi))
    for c, r0 in enumerate((0, NA_ROWS, rows - NA_ROWS)):
        ws = clamp(r0 - NA_KH // 2, 0, rows - NA_WIN)
        for j in range(NA_WIN):
            kr = ws + j
            for a0 in range(0, NA_ROWS, 2):
                halves = []
                for r in (r0 + a0, r0 + a0 + 1):
                    rs = clamp(r - NA_KH // 2, 0, rows - NA_KH)
                    halves.append(planes[kr - r + NA_KH - 1] if rs <= kr < rs + NA_KH else neg)
                o_ref[0, c, j * GRID_W:(j + 1) * GRID_W, a0 * GRID_W:(a0 + 2) * GRID_W] = (
                    jnp.where(lane < GRID_W, halves[0], halves[1]))


def _na_bias(rel_bias, rows):
    tab = rel_bias.reshape(C_HEADS, -1)
    shp = (3, NA_WIN * GRID_W, NA_ROWS * GRID_W)
    return pl.pallas_call(
        functools.partial(_na_bias_kernel, rows=rows),
        grid=(C_HEADS,),
        in_specs=[pl.BlockSpec(memory_space=pltpu.SMEM)],
        out_specs=pl.BlockSpec((1,) + shp, lambda h: (h, 0, 0, 0)),
        out_shape=jax.ShapeDtypeStruct((C_HEADS,) + shp, F32),
        compiler_params=_params(1),
        name="na_bias_tiles",
    )(tab)


def _na_kernel(q_ref, k_ref, v_ref, bias_ref, o_ref, s_scr, *, rows):
    nk = NA_WIN * GRID_W
    nq = NA_ROWS * GRID_W
    n_groups = rows // NA_ROWS
    hrow = lax.broadcasted_iota(jnp.int32, (2 * HEAD_DIM, nq), 0) // HEAD_DIM

    def window(g):
        ws = _na_window_start(g * NA_ROWS, rows)
        return pl.ds(pl.multiple_of(ws * GRID_W, 2 * GRID_W), nk)

    def produce(g, slot):
        qt = q_ref[0, :, pl.ds(pl.multiple_of(g * nq, nq), nq)].astype(F32)
        zero = jnp.zeros_like(qt)
        w = jnp.concatenate([jnp.where(hrow == hh, qt, zero) for hh in range(2)], axis=1).astype(BF16)
        case = jnp.where(g == 0, 0, jnp.where(g == n_groups - 1, 2, 1))
        bias = jnp.concatenate([bias_ref[hh, case] for hh in range(2)], axis=1)
        s_t = _dot(k_ref[0, window(g), :], w) + bias
        s_scr[slot] = s_t
        return jnp.max(s_t, axis=0, keepdims=True)

    def consume(g, slot, blk_max):
        p = jnp.exp2(s_scr[slot] - blk_max)
        ot = _dot(v_ref[0, 0, :, window(g)], p.astype(BF16))
        ot = ot[0:2 * HEAD_DIM] / ot[2 * HEAD_DIM:2 * HEAD_DIM + 1]
        o = jnp.concatenate([ot[hh * HEAD_DIM:(hh + 1) * HEAD_DIM, hh * nq:(hh + 1) * nq] for hh in range(2)],
                            axis=0)
        o_ref[0, pl.ds(pl.multiple_of(g * nq, nq), nq), :] = o.T.astype(BF16)

    per_trip = min(NA_GROUPS_PER_TRIP, n_groups)
    assert per_trip % 2 == 0 and n_groups % per_trip == 0

    def body(i, blk_max):
        g = per_trip * i
        for u in range(per_trip):
            nxt = produce(jnp.minimum(g + u + 1, n_groups - 1), (u + 1) % 2)
            consume(g + u, u % 2, blk_max)
            blk_max = nxt
        return blk_max

    lax.fori_loop(0, n_groups // per_trip, body, produce(0, 0))


def _na(qt, k, vt, bias):
    batch, seq, _ = k.shape
    rows = seq // GRID_W
    assert rows % 2 == 0 and rows >= NA_WIN + NA_ROWS
    nq = NA_ROWS * GRID_W
    pairs = C_HEADS // 2
    nat = pl.BlockSpec((1, seq, 2 * HEAD_DIM), lambda b, h: (b, 0, h))
    return pl.pallas_call(
        functools.partial(_na_kernel, rows=rows),
        grid=(batch, pairs),
        in_specs=[pl.BlockSpec((1, 2 * HEAD_DIM, seq), lambda b, h: (b, h, 0)),
                  nat,
                  pl.BlockSpec((1, 1, VC_ROWS, seq), lambda b, h: (b, h, 0, 0)),
                  pl.BlockSpec((2, 3, NA_WIN * GRID_W, nq), lambda b, h: (h, 0, 0, 0))],
        out_specs=nat,
        out_shape=jax.ShapeDtypeStruct((batch, seq, C_MIX), BF16),
        scratch_shapes=[pltpu.VMEM((2, NA_WIN * GRID_W, 2 * nq), F32)],
        compiler_params=_params(2),
        name="attn_nbr",
    )(qt, k, vt, bias)


def _outproj_kernel(*refs):
    n = (len(refs) - 2) // 2
    x_ref, o_ref = refs[0], refs[-1]
    y = x_ref[...]
    for a_ref, w_ref in zip(refs[1:1 + n], refs[1 + n:1 + 2 * n]):
        y = y + _dot(a_ref[...], w_ref[...])
    o_ref[...] = y


def _outproj(xf, mixes, weights):
    t = xf.shape[0]
    tile = lambda c: pl.BlockSpec((TM, c), lambda i: (i, 0))
    return pl.pallas_call(
        _outproj_kernel,
        grid=(t // TM,),
        in_specs=[tile(D_MODEL)] + [tile(m.shape[1]) for m in mixes] + [_resident(w.shape) for w in weights],
        out_specs=tile(D_MODEL),
        out_shape=jax.ShapeDtypeStruct((t, D_MODEL), F32),
        compiler_params=_params(1),
        name="out_proj",
    )(xf, *mixes, *weights)


def _ffn_kernel(xm_ref, xp_ref, xn_ref, g_ref, wu_ref, cw_ref, cb_ref, wd_ref, pg_ref, wg_ref, p_ref, wp_ref,
                fg_ref, o_ref, h_scr, acc_scr, u_scr, *, tiles_per_seq, final):
    i = pl.program_id(0)
    g = g_ref[...]
    xm = xm_ref[...]
    pos = i % tiles_per_seq
    hp = jnp.where(pos == 0, 0.0, _rms(xp_ref[...], g))
    hn = jnp.where(pos == tiles_per_seq - 1, 0.0, _rms(xn_ref[...], g))
    h_scr[...] = jnp.concatenate([hp, _rms(xm, g), hn], axis=0).astype(BF16)
    acc_scr[...] = jnp.zeros(acc_scr.shape, F32)

    def conv(u, col):
        cw = cw_ref[:, pl.ds(col, TF)]
        rows = u.shape[0]
        y = (pltpu.roll(u, 1, 0) * cw[0:1] + u * cw[1:2] + pltpu.roll(u, rows - 1, 0) * cw[2:3]
             + cb_ref[:, pl.ds(col, TF)])
        return y[HALO:HALO + TM_FFN]

    def up(c, slot):
        h = h_scr[...]
        u_scr[slot, 0] = _dot(h, wu_ref[:, pl.ds(c * TF, TF)])
        u_scr[slot, 1] = _dot(h, wu_ref[:, pl.ds(D_FF + c * TF, TF)])

    def act_down(c, slot):
        gate = conv(u_scr[slot, 0], c * TF)
        val = conv(u_scr[slot, 1], D_FF + c * TF)
        act = gate * jax.nn.sigmoid(gate) * val
        acc_scr[...] += _dot(act.astype(BF16), wd_ref[pl.ds(c * TF, TF), :])

    n_chunks = D_FF // TF
    n_slots = u_scr.shape[0]
    for c in range(min(n_slots - 1, n_chunks)):
        up(c, c)
    for c in range(n_chunks):
        ahead = c + n_slots - 1
        if ahead < n_chunks:
            up(ahead, ahead % n_slots)
        act_down(c, c % n_slots)

    x2 = xm + acc_scr[...]
    gate = jax.nn.sigmoid(_dot(_rms(x2, pg_ref[...]).astype(BF16), wg_ref[...]))
    x3 = x2 + gate * _dot(p_ref[...].astype(BF16), wp_ref[...])
    if final:
        x3 = _rms(x3, fg_ref[...])
    o_ref[...] = x3


def _ffn(xf, g, wu, cw, cb, wd, pg, wg, p, wp, fg, seq, final):
    t = xf.shape[0]
    tps = seq // TM_FFN
    nb = TM_FFN // HALO
    last = t // HALO - 1
    return pl.pallas_call(
        functools.partial(_ffn_kernel, tiles_per_seq=tps, final=final),
        grid=(t // TM_FFN,),
        in_specs=[pl.BlockSpec((TM_FFN, D_MODEL), lambda i: (i, 0)),
                  pl.BlockSpec((HALO, D_MODEL), lambda i: (jnp.maximum(i * nb - 1, 0), 0)),
                  pl.BlockSpec((HALO, D_MODEL), lambda i: (jnp.minimum((i + 1) * nb, last), 0)),
                  _resident((1, D_MODEL)),
                  _resident((D_MODEL, 2 * D_FF)),
                  _resident((3, 2 * D_FF)), _resident((1, 2 * D_FF)),
                  _resident((D_FF, D_MODEL)),
                  _resident((1, D_MODEL)),
                  _resident((D_MODEL, D_MODEL)),
                  pl.BlockSpec((TM_FFN, PLE_DIM), lambda i: (i, 0)),
                  _resident((PLE_DIM, D_MODEL)),
                  _resident((1, D_MODEL))],
        out_specs=pl.BlockSpec((TM_FFN, D_MODEL), lambda i: (i, 0)),
        out_shape=jax.ShapeDtypeStruct((t, D_MODEL), F32),
        scratch_shapes=[pltpu.VMEM((TM_FFN + 2 * HALO, D_MODEL), BF16), pltpu.VMEM((TM_FFN, D_MODEL), F32),
                        pltpu.VMEM((FFN_SLOTS, 2, TM_FFN + 2 * HALO, TF), F32)],
        compiler_params=_params(1),
        name="convffn_ple",
    )(xf, xf, xf, g, wu, cw, cb, wd, pg, wg, p, wp, fg)


def _rope_tables(seq):
    pos = jnp.arange(seq, dtype=jnp.int32)

    def tab(p, half, theta):
        inv = theta ** (-jnp.arange(half, dtype=F32) / half)
        ang = inv[:, None] * p.astype(F32)[None, :]
        return jnp.cos(ang), jnp.sin(ang)

    ca, sa = tab(pos, HEAD_DIM // 8, ROPE_THETA)
    cr, sr = tab(pos // GRID_W, HEAD_DIM // 4, AXIAL_THETA)
    cc, sc = tab(pos % GRID_W, HEAD_DIM // 4, AXIAL_THETA)
    return ca, sa, cr, sr, cc, sc


def kernel(x, p, attn_norm, w_in_ab, lambda_q1, lambda_k1, lambda_q2, lambda_k2, a_subln, b_q_norm, b_k_norm,
           w_out_ab, w_in_c, c_rel_bias, w_out_c, ffn_norm, w_ffn_up, ffn_conv_w, ffn_conv_b, w_ffn_down,
           ple_norm, w_ple_gate, w_ple_proj, final_norm):
    batch, seq, _ = x.shape
    depth = attn_norm.shape[0]
    t = batch * seq
    xf = x.reshape(t, D_MODEL)
    tabs = _rope_tables(seq)
    row = lambda v: v.reshape(1, -1)
    colv = lambda v: v.reshape(-1, 1)
    for i in range(depth):
        j = i // 2
        if i % 2 == 0:
            lam_init = 0.8 - 0.6 * math.exp(-0.3 * i)
            wt = w_in_ab[j].T.astype(BF16)
            qta, ka, vta, qtb, kb, vtb = _proj_even(xf, row(attn_norm[i]), wt, tabs, colv(b_q_norm[j]),
                                                    colv(b_k_norm[j]), batch, seq)
            mix_a = _attn_a(qta, ka, vta, row(lambda_q1[j]), row(lambda_k1[j]), row(lambda_q2[j]),
                            row(lambda_k2[j]), colv(a_subln[j]), lam_init)
            mix_b = _attn_b(qtb, kb, vtb)
            w_out = w_out_ab[j].astype(BF16)
            x1 = _outproj(xf, [mix_a.reshape(t, A_V), mix_b.reshape(t, B_Q)], [w_out[:A_V], w_out[A_V:]])
        else:
            w_c = w_in_c[j].astype(BF16)
            qt, k, vt = _proj_odd(xf, row(attn_norm[i]), w_c[:, :C_MIX].T, w_c[:, C_MIX:2 * C_MIX],
                                  w_c[:, 2 * C_MIX:].T, batch, seq)
            o = _na(qt, k, vt, _na_bias(c_rel_bias[j], seq // GRID_W))
            x1 = _outproj(xf, [o.reshape(t, C_MIX)], [w_out_c[j].astype(BF16)])
        xf = _ffn(x1, row(ffn_norm[i]), w_ffn_up[i].astype(BF16), ffn_conv_w[i], row(ffn_conv_b[i]),
                  w_ffn_down[i].astype(BF16), row(ple_norm[i]), w_ple_gate[i].astype(BF16),
                  p[i].reshape(t, PLE_DIM), w_ple_proj[i].astype(BF16), row(final_norm), seq,
                  final=(i == depth - 1))
    return xf.reshape(batch, seq, D_MODEL)
```

```python
import functools
import math

import jax
import jax.numpy as jnp
from jax import lax
from jax.experimental import pallas as pl
from jax.experimental.pallas import tpu as pltpu

F32 = jnp.float32
BF16 = jnp.bfloat16

D_MODEL = 1024
HEAD_DIM = 64
A_HEADS = 4
B_Q_HEADS = 8
B_KV_HEADS = 2
C_HEADS = 16
ROPE_THETA = 500000.0
AXIAL_THETA = 10000.0
GRID_W = 64
NA_KH = 8
NA_KW = 16
D_FF = 2816
PLE_DIM = 256
EPS = 1e-6
SCALE = HEAD_DIM ** -0.5
LOG2E = math.log2(math.e)
Q_SCALE = SCALE * LOG2E

A_QK = A_HEADS * 2 * HEAD_DIM
A_V = A_HEADS * 2 * HEAD_DIM
B_Q = B_Q_HEADS * HEAD_DIM
B_KV = B_KV_HEADS * HEAD_DIM
EVEN_IN = 2 * A_QK + A_V + B_Q + 2 * B_KV
C_MIX = C_HEADS * HEAD_DIM

TM = 1024
TQ_A = 512
TQ_B = 256
TK_A = 512
TK_B = 256
VA_ROWS = 2 * HEAD_DIM + 16
VB_ROWS = HEAD_DIM + 16
SCORE_SLOTS = 2
TM_FFN = 256
TF = 256
FFN_SLOTS = 4
HALO = 8
NA_ROWS = 4
NA_WIN = NA_ROWS + NA_KH
VC_ROWS = 2 * HEAD_DIM + 16
NA_GROUPS_PER_TRIP = 16
VMEM_LIMIT = 52 * 1024 * 1024

_NT = (((1,), (1,)), ((), ()))


def _dot(a, b):
    return jnp.dot(a, b, preferred_element_type=F32)


def _rms(xf, g):
    return xf * lax.rsqrt(jnp.mean(xf * xf, axis=-1, keepdims=True) + EPS) * g


def _params(n_axes, limit=VMEM_LIMIT):
    return pltpu.CompilerParams(dimension_semantics=("arbitrary",) * n_axes, vmem_limit_bytes=limit)


def _resident(shape):
    nd = len(shape)
    return pl.BlockSpec(shape, lambda *_: (0,) * nd, pipeline_mode=pl.Buffered(1))


def _proj_even_kernel(x_ref, g_ref, wt_ref, ca_ref, sa_ref, cr_ref, sr_ref, cc_ref, sc_ref, qn_ref, kn_ref,
                      qta_ref, ka_ref, vta_ref, qtb_ref, kb_ref, vtb_ref):
    h = _rms(x_ref[...], g_ref[...]).astype(BF16)

    def proj_t(lo, hi):
        return lax.dot_general(wt_ref[lo:hi, :], h, _NT, preferred_element_type=F32)

    ca, sa = ca_ref[...], sa_ref[...]
    cr, sr = cr_ref[...], sr_ref[...]
    cc, sc = cc_ref[...], sc_ref[...]

    def rope_a(blk):
        x1, x2 = blk[0:8], blk[8:16]
        return jnp.concatenate([x1 * ca - x2 * sa, x2 * ca + x1 * sa, blk[16:]], axis=0)

    def norm_rope_b(blk, gain):
        n = blk * lax.rsqrt(jnp.mean(blk * blk, axis=0, keepdims=True) + EPS) * gain
        r1, r2, c1, c2 = n[0:16], n[16:32], n[32:48], n[48:64]
        return jnp.concatenate([r1 * cr - r2 * sr, r2 * cr + r1 * sr,
                                c1 * cc - c2 * sc, c2 * cc + c1 * sc], axis=0)

    o = 0
    aq = proj_t(o, o + A_QK)
    qta_ref[0] = (jnp.concatenate([rope_a(aq[m * 64:(m + 1) * 64]) for m in range(2 * A_HEADS)], axis=0)
                  * Q_SCALE).astype(BF16)
    o += A_QK
    ak = proj_t(o, o + A_QK)
    for hh in range(A_HEADS):
        kt = jnp.concatenate([rope_a(ak[(2 * hh + m) * 64:(2 * hh + m + 1) * 64]) for m in range(2)], axis=0)
        ka_ref[0, hh] = kt.T.astype(BF16)
    o += A_QK
    av = proj_t(o, o + A_V)
    pad_a = jnp.concatenate([jnp.ones((8, TM), F32), jnp.zeros((8, TM), F32)], axis=0)
    for hh in range(A_HEADS):
        vta_ref[0, hh] = jnp.concatenate([av[hh * 128:(hh + 1) * 128], pad_a], axis=0).astype(BF16)
    o += A_V
    bq = proj_t(o, o + B_Q)
    qn = qn_ref[...]
    qtb_ref[0] = (jnp.concatenate([norm_rope_b(bq[m * 64:(m + 1) * 64], qn) for m in range(B_Q_HEADS)], axis=0)
                  * Q_SCALE).astype(BF16)
    o += B_Q
    bk = proj_t(o, o + B_KV)
    kn = kn_ref[...]
    kbt = jnp.concatenate([norm_rope_b(bk[m * 64:(m + 1) * 64], kn) for m in range(B_KV_HEADS)], axis=0)
    kb_ref[0] = kbt.T.astype(BF16)
    o += B_KV
    bv = proj_t(o, o + B_KV)
    pad = jnp.concatenate([jnp.ones((8, TM), F32), jnp.zeros((VB_ROWS - HEAD_DIM - 8, TM), F32)], axis=0)
    for gg in range(B_KV_HEADS):
        vtb_ref[0, gg] = jnp.concatenate([bv[gg * 64:(gg + 1) * 64], pad], axis=0).astype(BF16)


def _proj_even(xf, g, wt, tabs, qn, kn, batch, seq):
    t = xf.shape[0]
    ns = seq // TM
    tok = lambda i: (i // ns, 0, i % ns)
    tab = lambda rows: pl.BlockSpec((rows, TM), lambda i: (0, i % ns))
    out_shape = (
        jax.ShapeDtypeStruct((batch, A_QK, seq), BF16),
        jax.ShapeDtypeStruct((batch, A_HEADS, seq, 128), BF16),
        jax.ShapeDtypeStruct((batch, A_HEADS, VA_ROWS, seq), BF16),
        jax.ShapeDtypeStruct((batch, B_Q, seq), BF16),
        jax.ShapeDtypeStruct((batch, seq, B_KV), BF16),
        jax.ShapeDtypeStruct((batch, B_KV_HEADS, VB_ROWS, seq), BF16),
    )
    return pl.pallas_call(
        _proj_even_kernel,
        grid=(t // TM,),
        in_specs=[pl.BlockSpec((TM, D_MODEL), lambda i: (i, 0)),
                  _resident((1, D_MODEL)),
                  _resident((EVEN_IN, D_MODEL)),
                  tab(8), tab(8), tab(16), tab(16), tab(16), tab(16),
                  _resident((HEAD_DIM, 1)), _resident((HEAD_DIM, 1))],
        out_specs=(pl.BlockSpec((1, A_QK, TM), tok),
                   pl.BlockSpec((1, A_HEADS, TM, 128), lambda i: (i // ns, 0, i % ns, 0)),
                   pl.BlockSpec((1, A_HEADS, VA_ROWS, TM), lambda i: (i // ns, 0, 0, i % ns)),
                   pl.BlockSpec((1, B_Q, TM), tok),
                   pl.BlockSpec((1, TM, B_KV), lambda i: (i // ns, i % ns, 0)),
                   pl.BlockSpec((1, B_KV_HEADS, VB_ROWS, TM), lambda i: (i // ns, 0, 0, i % ns))),
        out_shape=out_shape,
        compiler_params=_params(1),
        name="proj_even",
    )(xf, g, wt, *tabs, qn, kn)


def _flash_tiles(load_w, scores_fn, values_fn, finalize_fn, n_tiles, n_blocks, w_scr, s_scr, p_scr, a_scr, m_scr,
                 acc_scr):
    n_slots = s_scr.shape[0]
    assert n_blocks % n_slots == 0

    def reset():
        m_scr[...] = jnp.full(m_scr.shape, -jnp.inf, F32)
        acc_scr[...] = jnp.zeros(acc_scr.shape, F32)

    def produce(j, slot, w):
        s_t = scores_fn(j, w)
        s_scr[slot] = s_t
        return jnp.max(s_t, axis=0, keepdims=True)

    def softmax(slot, blk_max):
        m_old = m_scr[...]
        m_new = jnp.maximum(m_old, blk_max)
        alpha = jnp.exp2(m_old - m_new)
        half = m_new.shape[1] // 2
        for c in (slice(0, half), slice(half, 2 * half)):
            p_scr[slot, :, c] = jnp.exp2(s_scr[slot, :, c] - m_new[:, c]).astype(BF16)
        a_scr[slot] = alpha
        m_scr[...] = m_new

    def accumulate(j, slot):
        acc_scr[...] = a_scr[slot] * acc_scr[...] + _dot(values_fn(j), p_scr[slot])

    def body(t, blk_max):
        cur = t % 2
        for u in range(n_blocks):
            if u + 1 < n_blocks:
                nxt = produce(u + 1, (u + 1) % n_slots, w_scr[cur])
            else:
                w_next = load_w(jnp.minimum(t + 1, n_tiles - 1))
                w_scr[1 - cur] = w_next
                nxt = produce(0, 0, w_next)
            softmax(u % n_slots, blk_max)
            if u >= 1:
                accumulate(u - 1, (u - 1) % n_slots)
            blk_max = nxt
        accumulate(n_blocks - 1, (n_blocks - 1) % n_slots)
        finalize_fn(t)
        reset()
        return blk_max

    reset()
    w_scr[0] = load_w(0)
    lax.fori_loop(0, n_tiles, body, produce(0, 0, w_scr[0]))


def _attn_a_kernel(q_ref, k_ref, v_ref, lq1_ref, lk1_ref, lq2_ref, lk2_ref, sub_ref, o_ref,
                   w_scr, s_scr, p_scr, a_scr, m_scr, acc_scr, *, lam_init, seq):
    tile = lambda t: pl.ds(pl.multiple_of(t * TQ_A, TQ_A), TQ_A)
    blk = lambda j: pl.ds(j * TK_A, TK_A)

    def load_w(t):
        qt = q_ref[0, :, tile(t)].astype(F32)
        row = lax.broadcasted_iota(jnp.int32, qt.shape, 0)
        zero = jnp.zeros_like(qt)
        return jnp.concatenate([jnp.where(row < HEAD_DIM, qt, zero), jnp.where(row >= HEAD_DIM, qt, zero)],
                               axis=1).astype(BF16)

    def finalize(t):
        o = acc_scr[0:128, :] / acc_scr[128:129, :]
        lam = (jnp.exp(jnp.sum(lq1_ref[...] * lk1_ref[...], axis=-1, keepdims=True))
               - jnp.exp(jnp.sum(lq2_ref[...] * lk2_ref[...], axis=-1, keepdims=True)) + lam_init)
        d = o[:, :TQ_A] - lam * o[:, TQ_A:]
        y = d * lax.rsqrt(jnp.mean(d * d, axis=0, keepdims=True) + EPS) * sub_ref[...]
        o_ref[0, tile(t), :] = (y * (1.0 - lam_init)).T.astype(BF16)

    _flash_tiles(load_w, lambda j, w: _dot(k_ref[0, 0, blk(j), :], w),
                 lambda j: v_ref[0, 0, :, blk(j)], finalize, seq // TQ_A, seq // TK_A,
                 w_scr, s_scr, p_scr, a_scr, m_scr, acc_scr)


def _attn_a(qta, ka, vta, lq1, lk1, lq2, lk2, subln, lam_init):
    batch, _, seq = qta.shape
    vec = _resident((1, HEAD_DIM))
    n = 2 * TQ_A
    return pl.pallas_call(
        functools.partial(_attn_a_kernel, lam_init=lam_init, seq=seq),
        grid=(batch, A_HEADS),
        in_specs=[pl.BlockSpec((1, 128, seq), lambda b, h: (b, h, 0)),
                  pl.BlockSpec((1, 1, seq, 128), lambda b, h: (b, h, 0, 0)),
                  pl.BlockSpec((1, 1, VA_ROWS, seq), lambda b, h: (b, h, 0, 0)),
                  vec, vec, vec, vec,
                  _resident((128, 1))],
        out_specs=pl.BlockSpec((1, seq, 128), lambda b, h: (b, 0, h)),
        out_shape=jax.ShapeDtypeStruct((batch, seq, A_V), BF16),
        scratch_shapes=[pltpu.VMEM((2, 128, n), BF16), pltpu.VMEM((SCORE_SLOTS, TK_A, n), F32),
                        pltpu.VMEM((SCORE_SLOTS, TK_A, n), BF16), pltpu.VMEM((SCORE_SLOTS, 1, n), F32),
                        pltpu.VMEM((1, n), F32), pltpu.VMEM((VA_ROWS, n), F32)],
        compiler_params=_params(2),
        name="attn_diff",
    )(qta, ka, vta, lq1, lk1, lq2, lk2, subln)


def _attn_b_kernel(q_ref, k_ref, v_ref, o_ref, w_scr, s_scr, p_scr, a_scr, m_scr, acc_scr, *, seq):
    g = pl.program_id(1)
    rep = B_Q_HEADS // B_KV_HEADS
    tile = lambda t: pl.ds(pl.multiple_of(t * TQ_B, TQ_B), TQ_B)
    blk = lambda j: pl.ds(j * TK_B, TK_B)

    def load_w(t):
        qt = q_ref[0, :, tile(t)].astype(F32)
        wq = jnp.concatenate([qt[r * 64:(r + 1) * 64] for r in range(rep)], axis=1)
        zero = jnp.zeros_like(wq)
        return jnp.concatenate([jnp.where(g == gg, wq, zero) for gg in range(B_KV_HEADS)], axis=0).astype(BF16)

    def finalize(t):
        o = acc_scr[0:HEAD_DIM, :] / acc_scr[HEAD_DIM:HEAD_DIM + 1, :]
        ot = jnp.concatenate([o[:, r * TQ_B:(r + 1) * TQ_B] for r in range(rep)], axis=0)
        o_ref[0, tile(t), :] = ot.T.astype(BF16)

    _flash_tiles(load_w, lambda j, w: _dot(k_ref[0, blk(j), :], w),
                 lambda j: v_ref[0, 0, :, blk(j)], finalize, seq // TQ_B, seq // TK_B,
                 w_scr, s_scr, p_scr, a_scr, m_scr, acc_scr)


def _attn_b(qtb, kb, vtb):
    batch, _, seq = qtb.shape
    rep = B_Q_HEADS // B_KV_HEADS
    n = rep * TQ_B
    return pl.pallas_call(
        functools.partial(_attn_b_kernel, seq=seq),
        grid=(batch, B_KV_HEADS),
        in_specs=[pl.BlockSpec((1, rep * HEAD_DIM, seq), lambda b, g: (b, g, 0)),
                  pl.BlockSpec((1, seq, B_KV), lambda b, g: (b, 0, 0)),
                  pl.BlockSpec((1, 1, VB_ROWS, seq), lambda b, g: (b, g, 0, 0))],
        out_specs=pl.BlockSpec((1, seq, rep * HEAD_DIM), lambda b, g: (b, 0, g)),
        out_shape=jax.ShapeDtypeStruct((batch, seq, B_Q), BF16),
        scratch_shapes=[pltpu.VMEM((2, B_KV, n), BF16), pltpu.VMEM((SCORE_SLOTS, TK_B, n), F32),
                        pltpu.VMEM((SCORE_SLOTS, TK_B, n), BF16), pltpu.VMEM((SCORE_SLOTS, 1, n), F32),
                        pltpu.VMEM((1, n), F32), pltpu.VMEM((VB_ROWS, n), F32)],
        compiler_params=_params(2),
        name="attn_gqa",
    )(qtb, kb, vtb)


def _proj_odd_kernel(x_ref, g_ref, wqt_ref, wk_ref, wvt_ref, qt_ref, k_ref, vt_ref):
    h = _rms(x_ref[...], g_ref[...]).astype(BF16)
    qt_ref[0] = (lax.dot_general(wqt_ref[...], h, _NT, preferred_element_type=F32) * Q_SCALE).astype(BF16)
    k_ref[0] = _dot(h, wk_ref[...]).astype(BF16)
    vt = lax.dot_general(wvt_ref[...], h, _NT, preferred_element_type=F32)
    pad = jnp.concatenate([jnp.ones((8, TM), F32), jnp.zeros((VC_ROWS - 2 * HEAD_DIM - 8, TM), F32)], axis=0)
    for pr in range(C_HEADS // 2):
        vt_ref[0, pr] = jnp.concatenate([vt[pr * 2 * HEAD_DIM:(pr + 1) * 2 * HEAD_DIM], pad], axis=0).astype(BF16)


def _proj_odd(xf, g, wqt, wk, wvt, batch, seq):
    t = xf.shape[0]
    ns = seq // TM
    tok_t = pl.BlockSpec((1, C_MIX, TM), lambda i: (i // ns, 0, i % ns))
    sq = _resident((C_MIX, D_MODEL))
    return pl.pallas_call(
        _proj_odd_kernel,
        grid=(t // TM,),
        in_specs=[pl.BlockSpec((TM, D_MODEL), lambda i: (i, 0)), _resident((1, D_MODEL)),
                  sq, _resident((D_MODEL, C_MIX)), sq],
        out_specs=(tok_t, pl.BlockSpec((1, TM, C_MIX), lambda i: (i // ns, i % ns, 0)),
                   pl.BlockSpec((1, C_HEADS // 2, VC_ROWS, TM), lambda i: (i // ns, 0, 0, i % ns))),
        out_shape=(jax.ShapeDtypeStruct((batch, C_MIX, seq), BF16),
                   jax.ShapeDtypeStruct((batch, seq, C_MIX), BF16),
                   jax.ShapeDtypeStruct((batch, C_HEADS // 2, VC_ROWS, seq), BF16)),
        compiler_params=_params(1),
        name="proj_odd",
    )(xf, g, wqt, wk, wvt)


def _na_window_start(r0, rows):
    return jnp.clip(r0 - NA_KH // 2, 0, rows - NA_WIN)


def _na_bias_kernel(tab_ref, o_ref, *, rows):
    h = pl.program_id(0)
    shp = (GRID_W, 2 * GRID_W)
    kc = lax.broadcasted_iota(jnp.int32, shp, 0)
    lane = lax.broadcasted_iota(jnp.int32, shp, 1)
    qc = lane & (GRID_W - 1)
    cs = jnp.clip(qc - NA_KW // 2, 0, GRID_W - NA_KW)
    col_valid = (kc >= cs) & (kc < cs + NA_KW)
    dci = jnp.clip(kc - qc + NA_KW - 1, 0, 2 * NA_KW - 2)
    neg = jnp.full(shp, -jnp.inf, F32)
    n_dc = 2 * NA_KW - 1
    n_dr = 2 * NA_KH - 1
    planes = []
    for dr in range(n_dr):
        acc = jnp.zeros(shp, F32)
        for dd in range(n_dc):
            acc = jnp.where(dci == dd, tab_ref[h, dr * n_dc + dd], acc)
        planes.append(jnp.where(col_valid, acc * LOG2E, neg))
    clamp = lambda v, lo, hi: max(lo, min(v, hi))
    for c, r0 in enumerate((0, NA_ROWS, rows - NA_ROWS)):
        ws = clamp(r0 - NA_KH // 2, 0, rows - NA_WIN)
        for j in range(NA_WIN):
            kr = ws + j
            for a0 in range(0, NA_ROWS, 2):
                halves = []
                for r in (r0 + a0, r0 + a0 + 1):
                    rs = clamp(r - NA_KH // 2, 0, rows - NA_KH)
                    halves.append(planes[kr - r + NA_KH - 1] if rs <= kr < rs + NA_KH else neg)
                o_ref[0, c, j * GRID_W:(j + 1) * GRID_W, a0 * GRID_W:(a0 + 2) * GRID_W] = (
                    jnp.where(lane < GRID_W, halves[0], halves[1]))


def _na_bias(rel_bias, rows):
    tab = rel_bias.reshape(C_HEADS, -1)
    shp = (3, NA_WIN * GRID_W, NA_ROWS * GRID_W)
    return pl.pallas_call(
        functools.partial(_na_bias_kernel, rows=rows),
        grid=(C_HEADS,),
        in_specs=[pl.BlockSpec(memory_space=pltpu.SMEM)],
        out_specs=pl.BlockSpec((1,) + shp, lambda h: (h, 0, 0, 0)),
        out_shape=jax.ShapeDtypeStruct((C_HEADS,) + shp, F32),
        compiler_params=_params(1),
        name="na_bias_tiles",
    )(tab)


def _na_kernel(q_ref, k_ref, v_ref, bias_ref, o_ref, s_scr, *, rows):
    nk = NA_WIN * GRID_W
    nq = NA_ROWS * GRID_W
    n_groups = rows // NA_ROWS
    hrow = lax.broadcasted_iota(jnp.int32, (2 * HEAD_DIM, nq), 0) // HEAD_DIM

    def window(g):
        ws = _na_window_start(g * NA_ROWS, rows)
        return pl.ds(pl.multiple_of(ws * GRID_W, 2 * GRID_W), nk)

    def produce(g, slot):
        qt = q_ref[0, :, pl.ds(pl.multiple_of(g * nq, nq), nq)].astype(F32)
        zero = jnp.zeros_like(qt)
        w = jnp.concatenate([jnp.where(hrow == hh, qt, zero) for hh in range(2)], axis=1).astype(BF16)
        case = jnp.where(g == 0, 0, jnp.where(g == n_groups - 1, 2, 1))
        bias = jnp.concatenate([bias_ref[hh, case] for hh in range(2)], axis=1)
        s_t = _dot(k_ref[0, window(g), :], w) + bias
        s_scr[slot] = s_t
        return jnp.max(s_t, axis=0, keepdims=True)

    def consume(g, slot, blk_max):
        p = jnp.exp2(s_scr[slot] - blk_max)
        ot = _dot(v_ref[0, 0, :, window(g)], p.astype(BF16))
        ot = ot[0:2 * HEAD_DIM] / ot[2 * HEAD_DIM:2 * HEAD_DIM + 1]
        o = jnp.concatenate([ot[hh * HEAD_DIM:(hh + 1) * HEAD_DIM, hh * nq:(hh + 1) * nq] for hh in range(2)],
                            axis=0)
        o_ref[0, pl.ds(pl.multiple_of(g * nq, nq), nq), :] = o.T.astype(BF16)

    per_trip = min(NA_GROUPS_PER_TRIP, n_groups)
    assert per_trip % 2 == 0 and n_groups % per_trip == 0

    def body(i, blk_max):
        g = per_trip * i
        for u in range(per_trip):
            nxt = produce(jnp.minimum(g + u + 1, n_groups - 1), (u + 1) % 2)
            consume(g + u, u % 2, blk_max)
            blk_max = nxt
        return blk_max

    lax.fori_loop(0, n_groups // per_trip, body, produce(0, 0))


def _na(qt, k, vt, bias):
    batch, seq, _ = k.shape
    rows = seq // GRID_W
    assert rows % 2 == 0 and rows >= NA_WIN + NA_ROWS
    nq = NA_ROWS * GRID_W
    pairs = C_HEADS // 2
    nat = pl.BlockSpec((1, seq, 2 * HEAD_DIM), lambda b, h: (b, 0, h))
    return pl.pallas_call(
        functools.partial(_na_kernel, rows=rows),
        grid=(batch, pairs),
        in_specs=[pl.BlockSpec((1, 2 * HEAD_DIM, seq), lambda b, h: (b, h, 0)),
                  nat,
                  pl.BlockSpec((1, 1, VC_ROWS, seq), lambda b, h: (b, h, 0, 0)),
                  pl.BlockSpec((2, 3, NA_WIN * GRID_W, nq), lambda b, h: (h, 0, 0, 0))],
        out_specs=nat,
        out_shape=jax.ShapeDtypeStruct((batch, seq, C_MIX), BF16),
        scratch_shapes=[pltpu.VMEM((2, NA_WIN * GRID_W, 2 * nq), F32)],
        compiler_params=_params(2),
        name="attn_nbr",
    )(qt, k, vt, bias)


def _outproj_kernel(*refs):
    n = (len(refs) - 2) // 2
    x_ref, o_ref = refs[0], refs[-1]
    y = x_ref[...]
    for a_ref, w_ref in zip(refs[1:1 + n], refs[1 + n:1 + 2 * n]):
        y = y + _dot(a_ref[...], w_ref[...])
    o_ref[...] = y


def _outproj(xf, mixes, weights):
    t = xf.shape[0]
    tile = lambda c: pl.BlockSpec((TM, c), lambda i: (i, 0))
    return pl.pallas_call(
        _outproj_kernel,
        grid=(t // TM,),
        in_specs=[tile(D_MODEL)] + [tile(m.shape[1]) for m in mixes] + [_resident(w.shape) for w in weights],
        out_specs=tile(D_MODEL),
        out_shape=jax.ShapeDtypeStruct((t, D_MODEL), F32),
        compiler_params=_params(1),
        name="out_proj",
    )(xf, *mixes, *weights)


def _ffn_kernel(xm_ref, xp_ref, xn_ref, g_ref, wu_ref, cw_ref, cb_ref, wd_ref, pg_ref, wg_ref, p_ref, wp_ref,
                fg_ref, o_ref, h_scr, acc_scr, u_scr, *, tiles_per_seq, final):
    i = pl.program_id(0)
    g = g_ref[...]
    xm = xm_ref[...]
    pos = i % tiles_per_seq
    hp = jnp.where(pos == 0, 0.0, _rms(xp_ref[...], g))
    hn = jnp.where(pos == tiles_per_seq - 1, 0.0, _rms(xn_ref[...], g))
    h_scr[...] = jnp.concatenate([hp, _rms(xm, g), hn], axis=0).astype(BF16)
    acc_scr[...] = jnp.zeros(acc_scr.shape, F32)

    def conv(u, col):
        cw = cw_ref[:, pl.ds(col, TF)]
        rows = u.shape[0]
        y = (pltpu.roll(u, 1, 0) * cw[0:1] + u * cw[1:2] + pltpu.roll(u, rows - 1, 0) * cw[2:3]
             + cb_ref[:, pl.ds(col, TF)])
        return y[HALO:HALO + TM_FFN]

    def up(c, slot):
        h = h_scr[...]
        u_scr[slot, 0] = _dot(h, wu_ref[:, pl.ds(c * TF, TF)])
        u_scr[slot, 1] = _dot(h, wu_ref[:, pl.ds(D_FF + c * TF, TF)])

    def act_down(c, slot):
        gate = conv(u_scr[slot, 0], c * TF)
        val = conv(u_scr[slot, 1], D_FF + c * TF)
        act = gate * jax.nn.sigmoid(gate) * val
        acc_scr[...] += _dot(act.astype(BF16), wd_ref[pl.ds(c * TF, TF), :])

    n_chunks = D_FF // TF
    n_slots = u_scr.shape[0]
    for c in range(min(n_slots - 1, n_chunks)):
        up(c, c)
    for c in range(n_chunks):
        ahead = c + n_slots - 1
        if ahead < n_chunks:
            up(ahead, ahead % n_slots)
        act_down(c, c % n_slots)

    x2 = xm + acc_scr[...]
    gate = jax.nn.sigmoid(_dot(_rms(x2, pg_ref[...]).astype(BF16), wg_ref[...]))
    x3 = x2 + gate * _dot(p_ref[...].astype(BF16), wp_ref[...])
    if final:
        x3 = _rms(x3, fg_ref[...])
    o_ref[...] = x3


def _ffn(xf, g, wu, cw, cb, wd, pg, wg, p, wp, fg, seq, final):
    t = xf.shape[0]
    tps = seq // TM_FFN
    nb = TM_FFN // HALO
    last = t // HALO - 1
    return pl.pallas_call(
        functools.partial(_ffn_kernel, tiles_per_seq=tps, final=final),
        grid=(t // TM_FFN,),
        in_specs=[pl.BlockSpec((TM_FFN, D_MODEL), lambda i: (i, 0)),
                  pl.BlockSpec((HALO, D_MODEL), lambda i: (jnp.maximum(i * nb - 1, 0), 0)),
                  pl.BlockSpec((HALO, D_MODEL), lambda i: (jnp.minimum((i + 1) * nb, last), 0)),
                  _resident((1, D_MODEL)),
                  _resident((D_MODEL, 2 * D_FF)),
                  _resident((3, 2 * D_FF)), _resident((1, 2 * D_FF)),
                  _resident((D_FF, D_MODEL)),
                  _resident((1, D_MODEL)),
                  _resident((D_MODEL, D_MODEL)),
                  pl.BlockSpec((TM_FFN, PLE_DIM), lambda i: (i, 0)),
                  _resident((PLE_DIM, D_MODEL)),
                  _resident((1, D_MODEL))],
        out_specs=pl.BlockSpec((TM_FFN, D_MODEL), lambda i: (i, 0)),
        out_shape=jax.ShapeDtypeStruct((t, D_MODEL), F32),
        scratch_shapes=[pltpu.VMEM((TM_FFN + 2 * HALO, D_MODEL), BF16), pltpu.VMEM((TM_FFN, D_MODEL), F32),
                        pltpu.VMEM((FFN_SLOTS, 2, TM_FFN + 2 * HALO, TF), F32)],
        compiler_params=_params(1),
        name="convffn_ple",
    )(xf, xf, xf, g, wu, cw, cb, wd, pg, wg, p, wp, fg)


def _rope_tables(seq):
    pos = jnp.arange(seq, dtype=jnp.int32)

    def tab(p, half, theta):
        inv = theta ** (-jnp.arange(half, dtype=F32) / half)
        ang = inv[:, None] * p.astype(F32)[None, :]
        return jnp.cos(ang), jnp.sin(ang)

    ca, sa = tab(pos, HEAD_DIM // 8, ROPE_THETA)
    cr, sr = tab(pos // GRID_W, HEAD_DIM // 4, AXIAL_THETA)
    cc, sc = tab(pos % GRID_W, HEAD_DIM // 4, AXIAL_THETA)
    return ca, sa, cr, sr, cc, sc


def kernel(x, p, attn_norm, w_in_ab, lambda_q1, lambda_k1, lambda_q2, lambda_k2, a_subln, b_q_norm, b_k_norm,
           w_out_ab, w_in_c, c_rel_bias, w_out_c, ffn_norm, w_ffn_up, ffn_conv_w, ffn_conv_b, w_ffn_down,
           ple_norm, w_ple_gate, w_ple_proj, final_norm):
    batch, seq, _ = x.shape
    depth = attn_norm.shape[0]
    t = batch * seq
    xf = x.reshape(t, D_MODEL)
    tabs = _rope_tables(seq)
    row = lambda v: v.reshape(1, -1)
    colv = lambda v: v.reshape(-1, 1)
    for i in range(depth):
        j = i // 2
        if i % 2 == 0:
            lam_init = 0.8 - 0.6 * math.exp(-0.3 * i)
            wt = w_in_ab[j].T.astype(BF16)
            qta, ka, vta, qtb, kb, vtb = _proj_even(xf, row(attn_norm[i]), wt, tabs, colv(b_q_norm[j]),
                                                    colv(b_k_norm[j]), batch, seq)
            mix_a = _attn_a(qta, ka, vta, row(lambda_q1[j]), row(lambda_k1[j]), row(lambda_q2[j]),
                            row(lambda_k2[j]), colv(a_subln[j]), lam_init)
            mix_b = _attn_b(qtb, kb, vtb)
            w_out = w_out_ab[j].astype(BF16)
            x1 = _outproj(xf, [mix_a.reshape(t, A_V), mix_b.reshape(t, B_Q)], [w_out[:A_V], w_out[A_V:]])
        else:
            w_c = w_in_c[j].astype(BF16)
            qt, k, vt = _proj_odd(xf, row(attn_norm[i]), w_c[:, :C_MIX].T, w_c[:, C_MIX:2 * C_MIX],
                                  w_c[:, 2 * C_MIX:].T, batch, seq)
            o = _na(qt, k, vt, _na_bias(c_rel_bias[j], seq // GRID_W))
            x1 = _outproj(xf, [o.reshape(t, C_MIX)], [w_out_c[j].astype(BF16)])
        xf = _ffn(x1, row(ffn_norm[i]), w_ffn_up[i].astype(BF16), ffn_conv_w[i], row(ffn_conv_b[i]),
                  w_ffn_down[i].astype(BF16), row(ple_norm[i]), w_ple_gate[i].astype(BF16),
                  p[i].reshape(t, PLE_DIM), w_ple_proj[i].astype(BF16), row(final_norm), seq,
                  final=(i == depth - 1))
    return xf.reshape(batch, seq, D_MODEL)
```
